```python
import jax, jax.numpy as jnp
from jax import lax
import numpy as np

D_MODEL = 2048
BATCH = 4
SEQ = 2048
DEPTH = 2
DEC_BATCH = 128
DEC_SEQ = 8
PAST_LEN = 16384
PAGE_SIZE = 128

E_A = D_MODEL // 2
E_B = D_MODEL // 2
E_C = D_MODEL // 2
CONV_W = 31
CHUNK = 128
B_GROUPS = 8
B_GC = E_B // B_GROUPS
POOL_WINDOWS = (2, 4, 8, 16)
C_GROUPS = len(POOL_WINDOWS)
C_GC = E_C // C_GROUPS
POOL_MAX = max(POOL_WINDOWS)
N_BRANCH = 3
N_IN = 3 * E_A + 3 * E_B + 2 * E_C + N_BRANCH * D_MODEL
EPS = 1e-6

kernel_name = "hybrid_conv_gmlp_pool_decoder_step"


def _rmsnorm(x, g):
    xf = x.astype(jnp.float32)
    y = xf * lax.rsqrt(jnp.mean(xf * xf, axis=-1, keepdims=True) + EPS)
    return (y * g.astype(jnp.float32)).astype(x.dtype)


def _layernorm(x, g, b):
    xf = x.astype(jnp.float32)
    mu = jnp.mean(xf, axis=-1, keepdims=True)
    xc = xf - mu
    var = jnp.mean(xc * xc, axis=-1, keepdims=True)
    return (xc * lax.rsqrt(var + EPS) * g.astype(jnp.float32) + b.astype(jnp.float32)).astype(x.dtype)


def _conv_module(a_val, a_gate, prefix, conv_w, conv_b, ln_g, ln_b):
    z = a_val * jax.nn.sigmoid(a_gate)
    zp = jnp.concatenate([prefix.astype(z.dtype), z], axis=1)
    y = lax.conv_general_dilated(
        zp, conv_w[:, None, :].astype(z.dtype), window_strides=(1,), padding='VALID',
        dimension_numbers=('NWC', 'WIO', 'NWC'), feature_group_count=E_A) + conv_b
    y = jax.nn.silu(_layernorm(y, ln_g, ln_b))
    return y, zp[:, -(CONV_W - 1):]


def _gmlp_module(u, v, ln_g, ln_b, w_s, b_s):
    v = _layernorm(v, ln_g, ln_b)
    n, t, _ = v.shape
    L = min(t, CHUNK)
    vc = v.reshape(n, t // L, L, B_GROUPS, B_GC)
    mask = jnp.tril(jnp.ones((L, L), dtype=bool))
    ws = jnp.where(mask[None], w_s[:, :L, :L], jnp.zeros((), w_s.dtype))
    mixed = jnp.einsum('gts,bnsgc->bntgc', ws, vc) + jnp.transpose(b_s[:, :L])[None, None, :, :, None]
    return u * mixed.reshape(n, t, E_B), v


def _pool_module(c, prefix, start, pool_w, pool_scale):
    n, t, _ = c.shape
    L = POOL_MAX - 1
    zc = jnp.concatenate([prefix.astype(c.dtype), c], axis=1)
    z = zc.astype(jnp.float32)
    cs = jnp.concatenate([jnp.zeros((n, 1, E_C), jnp.float32), lax.cumsum(z, axis=1)], axis=1)
    hi = cs[:, L + 1:]
    pos = start + jnp.arange(t)
    outs = []
    for g, w in enumerate(POOL_WINDOWS):
        sl = slice(g * C_GC, (g + 1) * C_GC)
        lo = cs[:, L + 1 - w:L + 1 - w + t, sl]
        cnt = jnp.minimum(w, pos + 1).astype(jnp.float32)[None, :, None]
        outs.append((hi[..., sl] - lo) / cnt)
    d = (jnp.concatenate(outs, axis=-1) - z[:, L:]).astype(c.dtype).reshape(n, t, C_GROUPS, C_GC)
    y = jnp.einsum('btgc,gcd->btgd', d, pool_w).reshape(n, t, E_C) * pool_scale
    return y, zc[:, -L:]


def _layer(x, conv_prefix, pool_prefix, start, g_pre, w_in, conv_w, conv_b, conv_ln_g, conv_ln_b,
           w_br_a, gmlp_ln_g, gmlp_ln_b, gmlp_ws, gmlp_bs, w_br_b, pool_w, pool_scale, w_br_c,
           w_out, g_post):
    n, t, _ = x.shape
    h = _rmsnorm(x, g_pre)
    proj = jnp.einsum('btd,df->btf', h, w_in)
    sizes = (E_A, E_A, E_A, E_B, E_B, E_B, E_C, E_C)
    idx = [int(i) for i in np.cumsum(sizes)]
    a_val, a_gate, a_silu, b_u, b_v, b_silu, c_in, c_silu, gate_logits = jnp.split(proj, idx, axis=-1)

    ya, conv_state = _conv_module(a_val, a_gate, conv_prefix, conv_w, conv_b, conv_ln_g, conv_ln_b)
    yb, v_rows = _gmlp_module(b_u, b_v, gmlp_ln_g, gmlp_ln_b, gmlp_ws, gmlp_bs)
    yc, pool_state = _pool_module(c_in, pool_prefix, start, pool_w, pool_scale)

    ya = jnp.einsum('bte,ed->btd', ya * jax.nn.silu(a_silu), w_br_a)
    yb = jnp.einsum('bte,ed->btd', yb * jax.nn.silu(b_silu), w_br_b)
    yc = jnp.einsum('bte,ed->btd', yc * jax.nn.silu(c_silu), w_br_c)

    gates = jax.nn.sigmoid(gate_logits).reshape(n, t, N_BRANCH, D_MODEL)
    m = gates[:, :, 0] * ya + gates[:, :, 1] * yb + gates[:, :, 2] * yc
    y = jnp.einsum('btd,de->bte', m, w_out)
    return x + _rmsnorm(y, g_post), conv_state, pool_state, v_rows


def setup_inputs(seed: int = 0) -> dict:
    key = jax.random.key(seed)
    ks = jax.random.split(key, 24)
    f32 = jnp.float32
    nrm = lambda k, shape, s: jax.random.normal(k, shape, f32) * s
    return {
        "x_prompt": nrm(ks[0], (BATCH, SEQ, D_MODEL), 1.0),
        "x_sample": nrm(ks[1], (DEC_BATCH, DEC_SEQ, D_MODEL), 1.0),
        "state_conv": nrm(ks[2], (DEPTH, DEC_BATCH, CONV_W - 1, E_A), 0.5),
        "state_pool": nrm(ks[3], (DEPTH, DEC_BATCH, POOL_MAX - 1, E_C), 1.0),
        "g_pre": 1.0 + nrm(ks[4], (DEPTH, D_MODEL), 0.02),
        "w_in": nrm(ks[5], (DEPTH, D_MODEL, N_IN), D_MODEL ** -0.5),
        "conv_w": nrm(ks[6], (DEPTH, CONV_W, E_A), CONV_W ** -0.5),
        "conv_b": nrm(ks[7], (DEPTH, E_A), 0.02),
        "conv_ln_g": 1.0 + nrm(ks[8], (DEPTH, E_A), 0.02),
        "conv_ln_b": nrm(ks[9], (DEPTH, E_A), 0.02),
        "w_br_a": nrm(ks[10], (DEPTH, E_A, D_MODEL), E_A ** -0.5),
        "gmlp_ln_g": 1.0 + nrm(ks[11], (DEPTH, E_B), 0.02),
        "gmlp_ln_b": nrm(ks[12], (DEPTH, E_B), 0.02),
        "gmlp_ws": nrm(ks[13], (DEPTH, B_GROUPS, CHUNK, CHUNK), CHUNK ** -0.5),
        "gmlp_bs": 1.0 + nrm(ks[14], (DEPTH, B_GROUPS, CHUNK), 0.02),
        "w_br_b": nrm(ks[15], (DEPTH, E_B, D_MODEL), E_B ** -0.5),
        "pool_w": nrm(ks[16], (DEPTH, C_GROUPS, C_GC, C_GC), C_GC ** -0.5),
        "pool_scale": 1.0 + nrm(ks[17], (DEPTH, E_C), 0.02),
        "w_br_c": nrm(ks[18], (DEPTH, E_C, D_MODEL), E_C ** -0.5),
        "w_out": nrm(ks[19], (DEPTH, D_MODEL, D_MODEL), D_MODEL ** -0.5),
        "g_post": 1.0 + nrm(ks[20], (DEPTH, D_MODEL), 0.02),
    }


def reference(x_prompt, x_sample, state_conv, state_pool, g_pre, w_in, conv_w, conv_b, conv_ln_g,
              conv_ln_b, w_br_a, gmlp_ln_g, gmlp_ln_b, gmlp_ws, gmlp_bs, w_br_b, pool_w, pool_scale,
              w_br_c, w_out, g_post):
    xp, xs = x_prompt, x_sample
    conv_p, pool_p, conv_s, pool_s, v_s = [], [], [], [], []
    zero_conv = jnp.zeros((BATCH, CONV_W - 1, E_A), x_prompt.dtype)
    zero_pool = jnp.zeros((BATCH, POOL_MAX - 1, E_C), x_prompt.dtype)
    for l in range(DEPTH):
        w = (g_pre[l], w_in[l], conv_w[l], conv_b[l], conv_ln_g[l], conv_ln_b[l], w_br_a[l],
             gmlp_ln_g[l], gmlp_ln_b[l], gmlp_ws[l], gmlp_bs[l], w_br_b[l], pool_w[l],
             pool_scale[l], w_br_c[l], w_out[l], g_post[l])
        xp, cst, pst, _ = _layer(xp, zero_conv, zero_pool, 0, *w)
        xs, cst_s, pst_s, v_rows = _layer(xs, state_conv[l], state_pool[l], PAST_LEN, *w)
        conv_p.append(cst); pool_p.append(pst)
        conv_s.append(cst_s); pool_s.append(pst_s); v_s.append(v_rows)
    new_conv_prompt = jnp.stack(conv_p)
    new_pool_prompt = jnp.stack(pool_p)
    new_conv_sample = jnp.stack(conv_s)
    new_pool_sample = jnp.stack(pool_s)
    new_gmlp_v_sample = jnp.stack(v_s)
    return (xp, xs, new_conv_prompt, new_pool_prompt, new_conv_sample, new_pool_sample, new_gmlp_v_sample)
```

```python
import functools

import jax
import jax.numpy as jnp
import numpy as np
from jax import lax
from jax.experimental import pallas as pl
from jax.experimental.pallas import tpu as pltpu

D_MODEL = 2048
E = 1024
CONV_W = 31
CONV_HALO = 32
CHUNK = 128
B_GROUPS = 8
B_GC = E // B_GROUPS
POOL_WINDOWS = (2, 4, 8, 16)
C_GC = E // len(POOL_WINDOWS)
POOL_MAX = 16
POOL_HALO = 16
DEC_SEQ = 8
PAST_LEN = 16384
EPS = 1e-6
N_STEPS = 16
COL = 1024

TM_PROMPT = 512
TM_SAMPLE = 256
VMEM_LIMIT_BYTES = 58 * 1024 * 1024

_W_IN_BLOCK_ORDER = (1, 0, 2, 8, 9, 4, 3, 5, 10, 11, 6, 7, 12, 13)

f32 = jnp.float32
bf16 = jnp.bfloat16


def _for_rows(n_rows, rc, fn):
    def body(c, carry):
        fn(pl.multiple_of(c * rc, rc))
        return carry
    lax.fori_loop(0, n_rows // rc, body, 0)


def _sigmoid(x):
    return jax.nn.sigmoid(x)


def _layernorm_rows(y, g, b):
    mu = jnp.mean(y, axis=-1, keepdims=True)
    yc = y - mu
    var = jnp.mean(yc * yc, axis=-1, keepdims=True)
    return yc * lax.rsqrt(var + EPS) * g + b


def _pre_norm(x_ref, vd_ref, lhs_ref, tm):
    def go(r0):
        r = pl.ds(r0, 64)
        x = x_ref[r, :]
        ms = jnp.mean(x * x, axis=-1, keepdims=True)
        lhs_ref[r, :] = (x * lax.rsqrt(ms + EPS) * vd_ref[0:1, :]).astype(bf16)
    _for_rows(tm, 64, go)


def _store_sigmoid(p_ref, s1_ref, tm):
    def go(r0):
        r = pl.ds(r0, 64)
        s1_ref[r, :] = _sigmoid(p_ref[r, :])
    _for_rows(tm, 64, go)


def _mul_into_s1(p_ref, s1_ref, tm):
    def go(r0):
        r = pl.ds(r0, 64)
        s1_ref[r, :] = p_ref[r, :] * s1_ref[r, :]
    _for_rows(tm, 64, go)


def _widen(p_ref, s1_ref, su_ref, tm):
    def go(r0):
        r = pl.ds(r0, 64)
        p = p_ref[r, :]
        su_ref[r, :] = (s1_ref[r, :] * (p * _sigmoid(p))).astype(bf16)
    _for_rows(tm, 64, go)


def _merge(p_ref, su_ref, w2_ref, m_ref, half, first):
    cols = slice(half * COL, (half + 1) * COL)
    y = jnp.dot(su_ref[...], w2_ref[...], preferred_element_type=f32)
    g = _sigmoid(p_ref[...]) * y
    if first:
        m_ref[:, cols] = g
    else:
        m_ref[:, cols] += g


def _m_to_lhs(m_ref, lhs_ref, tm):
    def go(r0):
        r = pl.ds(r0, 64)
        lhs_ref[r, :] = m_ref[r, :].astype(bf16)
    _for_rows(tm, 64, go)


def _copy_p_to_s1(p_ref, s1_ref, tm):
    def go(r0):
        r = pl.ds(r0, 64)
        s1_ref[r, :] = p_ref[r, :]
    _for_rows(tm, 64, go)


def _post_norm_residual(x_ref, p_ref, s1_ref, vd_ref, out_ref, tm):
    def go(r0):
        r = pl.ds(r0, 64)
        y0 = s1_ref[r, :]
        y1 = p_ref[r, :]
        ss = jnp.sum(y0 * y0, axis=-1, keepdims=True) + jnp.sum(y1 * y1, axis=-1, keepdims=True)
        inv = lax.rsqrt(ss / D_MODEL + EPS)
        out_ref[r, 0:COL] = x_ref[r, 0:COL] + y0 * inv * vd_ref[1:2, 0:COL]
        out_ref[r, COL:D_MODEL] = x_ref[r, COL:D_MODEL] + y1 * inv * vd_ref[1:2, COL:D_MODEL]
    _for_rows(tm, 64, go)


def _ln_rows_into_s1(src_ref, s1_ref, g, b, tm, swish):
    def go(r0):
        r = pl.ds(r0, 64)
        y = _layernorm_rows(src_ref[r, :], g, b)
        s1_ref[r, :] = y * _sigmoid(y) if swish else y
    _for_rows(tm, 64, go)


def _common_tail_steps(s, x_ref, w2_ref, vd_ref, lhs_ref, p_ref, s1_ref, su_ref, m_ref, out_ref, tm):
    @pl.when(s == 0)
    def _():
        _store_sigmoid(p_ref, s1_ref, tm)

    for step in (2, 7, 11):
        @pl.when(s == step)
        def _():
            _widen(p_ref, s1_ref, su_ref, tm)

    for step, half, first in ((3, 0, True), (4, 1, True), (8, 0, False), (9, 1, False),
                              (12, 0, False), (13, 1, False)):
        @pl.when(s == step)
        def _(half=half, first=first):
            _merge(p_ref, su_ref, w2_ref, m_ref, half, first)

    @pl.when(s == 6)
    def _():
        _mul_into_s1(p_ref, s1_ref, tm)

    @pl.when(s == 13)
    def _():
        _m_to_lhs(m_ref, lhs_ref, tm)

    @pl.when(s == 14)
    def _():
        _copy_p_to_s1(p_ref, s1_ref, tm)

    @pl.when(s == 15)
    def _():
        _post_norm_residual(x_ref, p_ref, s1_ref, vd_ref, out_ref, tm)


def _prompt_kernel(x_ref, w_ref, w2_ref, vd_ref, ve_ref, cw_ref, ws_ref, bs_ref, pw_ref,
                   out_ref, cst_ref, pst_ref,
                   lhs_ref, p_ref, s1_ref, su_ref, m_ref, zbuf_ref, cbuf_ref,
                   *, tiles_per_seq):
    tm = TM_PROMPT
    i = pl.program_id(0)
    s = pl.program_id(1)
    q = i % tiles_per_seq

    @pl.when(s == 0)
    def _():
        _pre_norm(x_ref, vd_ref, lhs_ref, tm)

    p_ref[...] = jnp.dot(lhs_ref[...], w_ref[...], preferred_element_type=f32)

    _common_tail_steps(s, x_ref, w2_ref, vd_ref, lhs_ref, p_ref, s1_ref, su_ref, m_ref, out_ref, tm)

    @pl.when(s == 1)
    def _():
        @pl.when(q == 0)
        def _():
            zbuf_ref[0:CONV_HALO, :] = jnp.zeros((CONV_HALO, E), f32)

        def glu(r0):
            zbuf_ref[pl.ds(CONV_HALO + r0, 64), :] = p_ref[pl.ds(r0, 64), :] * s1_ref[pl.ds(r0, 64), :]
        _for_rows(tm, 64, glu)

        def conv(r0):
            for cb in range(E // 256):
                cols = slice(cb * 256, (cb + 1) * 256)
                acc = jnp.broadcast_to(ve_ref[0:1, cols], (64, 256))
                blk = zbuf_ref[pl.ds(r0, 64 + CONV_HALO), cols]
                for k in range(CONV_W):
                    off = CONV_HALO - (CONV_W - 1) + k
                    acc = acc + cw_ref[k:k + 1, cols] * blk[off:off + 64, :]
                s1_ref[pl.ds(r0, 64), cols] = acc
        _for_rows(tm, 64, conv)

        _ln_rows_into_s1(s1_ref, s1_ref, ve_ref[1:2, :], ve_ref[2:3, :], tm, swish=True)

        cst_ref[...] = zbuf_ref[CONV_HALO + tm - (CONV_W - 1):CONV_HALO + tm, :]
        zbuf_ref[0:CONV_HALO, :] = zbuf_ref[tm:tm + CONV_HALO, :]

    @pl.when(s == 5)
    def _():
        _ln_rows_into_s1(p_ref, s1_ref, ve_ref[3:4, :], ve_ref[4:5, :], tm, swish=False)
        tril = (lax.broadcasted_iota(jnp.int32, (CHUNK, CHUNK), 0)
                >= lax.broadcasted_iota(jnp.int32, (CHUNK, CHUNK), 1))
        for g in range(B_GROUPS):
            cols = slice(g * B_GC, (g + 1) * B_GC)
            w_g = jnp.where(tril, ws_ref[g], 0.0).astype(bf16)
            for c in range(tm // CHUNK):
                rows = slice(c * CHUNK, (c + 1) * CHUNK)
                v = s1_ref[rows, cols].astype(bf16)
                s1_ref[rows, cols] = (jnp.dot(w_g, v, preferred_element_type=f32) + bs_ref[:, cols])

    @pl.when(s == 10)
    def _():
        @pl.when(q == 0)
        def _():
            cbuf_ref[0:POOL_HALO, :] = jnp.zeros((POOL_HALO, E), f32)

        def stash(r0):
            cbuf_ref[pl.ds(POOL_HALO + r0, 64), :] = p_ref[pl.ds(r0, 64), :]
        _for_rows(tm, 64, stash)

        def pool(r0):
            pos = q * tm + r0 + lax.broadcasted_iota(jnp.int32, (CHUNK, C_GC), 0)
            for gi, w in enumerate(POOL_WINDOWS):
                cols = slice(gi * C_GC, (gi + 1) * C_GC)
                blk = cbuf_ref[pl.ds(r0, POOL_HALO + CHUNK), cols]
                c = blk[POOL_HALO:, :]
                win = c
                for j in range(1, w):
                    win = win + blk[POOL_HALO - j:POOL_HALO - j + CHUNK, :]
                cnt = jnp.minimum(w, pos + 1).astype(f32)
                d = win / cnt - c
                y = jnp.dot(d.astype(bf16), pw_ref[gi], preferred_element_type=f32)
                s1_ref[pl.ds(r0, CHUNK), cols] = y * ve_ref[5:6, cols]
        _for_rows(tm, CHUNK, pool)

        pst_ref[...] = cbuf_ref[POOL_HALO + tm - (POOL_MAX - 1):POOL_HALO + tm, :]
        cbuf_ref[0:POOL_HALO, :] = cbuf_ref[tm:tm + POOL_HALO, :]


def _sample_kernel(x_ref, w_ref, w2_ref, vd_ref, ve_ref, cw_ref, coef_ref, b8_ref, pw_ref,
                   stc_ref, stp_ref,
                   out_ref, z_ref, c_ref, v_ref,
                   lhs_ref, p_ref, s1_ref, su_ref, m_ref, tail_ref):
    tm = TM_SAMPLE
    n_seq = tm // DEC_SEQ
    sb = 8
    s = pl.program_id(1)

    @pl.when(s == 0)
    def _():
        _pre_norm(x_ref, vd_ref, lhs_ref, tm)

    p_ref[...] = jnp.dot(lhs_ref[...], w_ref[...], preferred_element_type=f32)

    _common_tail_steps(s, x_ref, w2_ref, vd_ref, lhs_ref, p_ref, s1_ref, su_ref, m_ref, out_ref, tm)

    def rows_of(q0):
        return pl.ds(pl.multiple_of(q0 * DEC_SEQ, sb * DEC_SEQ), sb * DEC_SEQ)

    @pl.when(s == 1)
    def _():
        def go(q0):
            qs = pl.ds(q0, sb)
            r = rows_of(q0)
            z = (p_ref[r, :] * s1_ref[r, :]).reshape(sb, DEC_SEQ, E)
            z_ref[qs, :, :] = z
            tail_ref[qs, 0:8, :] = stc_ref[qs, CONV_W - 1 - 8:CONV_W - 1, :]
            tail_ref[qs, 8:16, :] = z
            for cb in range(E // 256):
                cols = slice(cb * 256, (cb + 1) * 256)
                acc = jnp.broadcast_to(ve_ref[0:1, cols].reshape(1, 1, 256), (sb, DEC_SEQ, 256))
                for k in range(CONV_W):
                    wk = cw_ref[k:k + 1, cols].reshape(1, 1, 256)
                    if k + DEC_SEQ <= CONV_W - 1:
                        win = stc_ref[qs, k:k + DEC_SEQ, cols]
                    else:
                        o = k - (CONV_W - 1 - 8)
                        win = tail_ref[qs, o:o + DEC_SEQ, cols]
                    acc = acc + wk * win
                s1_ref[r, cols] = acc.reshape(sb * DEC_SEQ, 256)
        _for_rows(n_seq, sb, go)
        _ln_rows_into_s1(s1_ref, s1_ref, ve_ref[1:2, :], ve_ref[2:3, :], tm, swish=True)

    @pl.when(s == 5)
    def _():
        _ln_rows_into_s1(p_ref, s1_ref, ve_ref[3:4, :], ve_ref[4:5, :], tm, swish=False)

        def go(q0):
            qs = pl.ds(q0, sb)
            r = rows_of(q0)
            v = s1_ref[r, :].reshape(sb, DEC_SEQ, E)
            v_ref[qs, :, :] = v
            tail_ref[qs, 0:8, :] = jnp.zeros((sb, 8, E), f32)
            tail_ref[qs, 8:16, :] = v
            for cb in range(E // 256):
                cols = slice(cb * 256, (cb + 1) * 256)
                acc = jnp.broadcast_to(b8_ref[:, cols].reshape(1, DEC_SEQ, 256), (sb, DEC_SEQ, 256))
                for j in range(DEC_SEQ):
                    acc = acc + coef_ref[j, :, cols].reshape(1, DEC_SEQ, 256) * tail_ref[qs, 8 - j:16 - j, cols]
                s1_ref[r, cols] = acc.reshape(sb * DEC_SEQ, 256)
        _for_rows(n_seq, sb, go)

    @pl.when(s == 10)
    def _():
        def go(q0):
            qs = pl.ds(q0, sb)
            r = rows_of(q0)
            c3 = p_ref[r, :].reshape(sb, DEC_SEQ, E)
            c_ref[qs, :, :] = c3
            tail_ref[qs, 0:8, :] = stp_ref[qs, POOL_MAX - 1 - 8:POOL_MAX - 1, :]
            tail_ref[qs, 8:16, :] = c3
            for gi, w in enumerate(POOL_WINDOWS):
                cols = slice(gi * C_GC, (gi + 1) * C_GC)
                c = tail_ref[qs, 8:16, cols]
                win = c
                for j in range(1, w):
                    if j < 8:
                        win = win + tail_ref[qs, 8 - j:16 - j, cols]
                    else:
                        win = win + stp_ref[qs, POOL_MAX - 1 - j:POOL_MAX - 1 - j + DEC_SEQ, cols]
                d = win / float(min(w, PAST_LEN + 1)) - c
                s1_ref[r, cols] = d.reshape(sb * DEC_SEQ, C_GC)
        _for_rows(n_seq, sb, go)

        for gi in range(len(POOL_WINDOWS)):
            cols = slice(gi * C_GC, (gi + 1) * C_GC)
            y = jnp.dot(s1_ref[:, cols].astype(bf16), pw_ref[gi], preferred_element_type=f32)
            s1_ref[:, cols] = y * ve_ref[5:6, cols]


def _w2_index(s):
    one = jnp.int32(1)
    zero = jnp.int32(0)
    return (jnp.where(s >= 4, one, zero) + jnp.where(s >= 5, one, zero) + jnp.where(s >= 9, one, zero)
            + jnp.where(s >= 10, one, zero) + jnp.where(s >= 13, one, zero))


def _weight_specs(l):
    return [
        pl.BlockSpec((None, D_MODEL, COL), lambda i, s: (l, 0, s)),
        pl.BlockSpec((None, None, E, COL), lambda i, s: (l, _w2_index(s), 0, 0)),
        pl.BlockSpec((None, 2, D_MODEL), lambda i, s: (l, 0, 0)),
        pl.BlockSpec((None, 6, E), lambda i, s: (l, 0, 0)),
        pl.BlockSpec((None, CONV_W, E), lambda i, s: (l, 0, 0)),
    ]


def _compiler_params():
    return pltpu.CompilerParams(dimension_semantics=("arbitrary", "arbitrary"),
                                vmem_limit_bytes=VMEM_LIMIT_BYTES)


def _prompt_layer(l, x2d, n_seq, seq_len, wcat, w2, vd, ve, cw, ws, bs_full, pw):
    tm = TM_PROMPT
    assert seq_len % tm == 0 and tm % CHUNK == 0
    tiles_per_seq = seq_len // tm
    n_rows = n_seq * seq_len
    grid = (n_rows // tm, N_STEPS)
    in_specs = [pl.BlockSpec((tm, D_MODEL), lambda i, s: (i, 0))] + _weight_specs(l) + [
        pl.BlockSpec((None, B_GROUPS, CHUNK, CHUNK), lambda i, s: (l, 0, 0, 0)),
        pl.BlockSpec((None, CHUNK, E), lambda i, s: (l, 0, 0)),
        pl.BlockSpec((None, len(POOL_WINDOWS), C_GC, C_GC), lambda i, s: (l, 0, 0, 0)),
    ]
    out_specs = [
        pl.BlockSpec((tm, D_MODEL), lambda i, s: (i, 0)),
        pl.BlockSpec((None, CONV_W - 1, E), lambda i, s: (i // tiles_per_seq, 0, 0)),
        pl.BlockSpec((None, POOL_MAX - 1, E), lambda i, s: (i // tiles_per_seq, 0, 0)),
    ]
    out_shape = [
        jax.ShapeDtypeStruct((n_rows, D_MODEL), f32),
        jax.ShapeDtypeStruct((n_seq, CONV_W - 1, E), f32),
        jax.ShapeDtypeStruct((n_seq, POOL_MAX - 1, E), f32),
    ]
    scratch = [
        pltpu.VMEM((tm, D_MODEL), bf16),
        pltpu.VMEM((tm, COL), f32),
        pltpu.VMEM((tm, E), f32),
        pltpu.VMEM((tm, E), bf16),
        pltpu.VMEM((tm, D_MODEL), f32),
        pltpu.VMEM((CONV_HALO + tm, E), f32),
        pltpu.VMEM((POOL_HALO + tm, E), f32),
    ]
    return pl.pallas_call(
        functools.partial(_prompt_kernel, tiles_per_seq=tiles_per_seq),
        grid=grid, in_specs=in_specs, out_specs=out_specs, out_shape=out_shape,
        scratch_shapes=scratch, compiler_params=_compiler_params(),
        name=f"prompt_layer{l}",
    )(x2d, wcat, w2, vd, ve, cw, ws, bs_full, pw)


def _sample_layer(l, x2d, wcat, w2, vd, ve, cw, coef, b8, pw, state_conv, state_pool):
    tm = TM_SAMPLE
    n_rows = x2d.shape[0]
    n_seq_total = n_rows // DEC_SEQ
    sq = tm // DEC_SEQ
    assert n_rows % tm == 0
    grid = (n_rows // tm, N_STEPS)
    in_specs = [pl.BlockSpec((tm, D_MODEL), lambda i, s: (i, 0))] + _weight_specs(l) + [
        pl.BlockSpec((None, DEC_SEQ, DEC_SEQ, E), lambda i, s: (l, 0, 0, 0)),
        pl.BlockSpec((None, DEC_SEQ, E), lambda i, s: (l, 0, 0)),
        pl.BlockSpec((None, len(POOL_WINDOWS), C_GC, C_GC), lambda i, s: (l, 0, 0, 0)),
        pl.BlockSpec((None, sq, CONV_W - 1, E), lambda i, s: (l, i, 0, 0)),
        pl.BlockSpec((None, sq, POOL_MAX - 1, E), lambda i, s: (l, i, 0, 0)),
    ]
    out_specs = [
        pl.BlockSpec((tm, D_MODEL), lambda i, s: (i, 0)),
        pl.BlockSpec((sq, DEC_SEQ, E), lambda i, s: (i, 0, 0)),
        pl.BlockSpec((sq, DEC_SEQ, E), lambda i, s: (i, 0, 0)),
        pl.BlockSpec((sq, DEC_SEQ, E), lambda i, s: (i, 0, 0)),
    ]
    out_shape = [
        jax.ShapeDtypeStruct((n_rows, D_MODEL), f32),
        jax.ShapeDtypeStruct((n_seq_total, DEC_SEQ, E), f32),
        jax.ShapeDtypeStruct((n_seq_total, DEC_SEQ, E), f32),
        jax.ShapeDtypeStruct((n_seq_total, DEC_SEQ, E), f32),
    ]
    scratch = [
        pltpu.VMEM((tm, D_MODEL), bf16),
        pltpu.VMEM((tm, COL), f32),
        pltpu.VMEM((tm, E), f32),
        pltpu.VMEM((tm, E), bf16),
        pltpu.VMEM((tm, D_MODEL), f32),
        pltpu.VMEM((sq, 16, E), f32),
    ]
    return pl.pallas_call(
        _sample_kernel,
        grid=grid, in_specs=in_specs, out_specs=out_specs, out_shape=out_shape,
        scratch_shapes=scratch, compiler_params=_compiler_params(),
        name=f"sample_layer{l}",
    )(x2d, wcat, w2, vd, ve, cw, coef, b8, pw, state_conv, state_pool)


def _sample_mix_tables(gmlp_ws, gmlp_bs):
    t = np.arange(DEC_SEQ)
    pick = (t[None, None, :] == t[None, :, None] - t[:, None, None]).astype(np.float32)
    w = jnp.sum(gmlp_ws[:, :, None, :DEC_SEQ, :DEC_SEQ] * pick[None, None], axis=-1)
    coef = jnp.repeat(jnp.transpose(w, (0, 2, 3, 1)), B_GC, axis=-1)
    b8 = jnp.repeat(jnp.transpose(gmlp_bs[:, :, :DEC_SEQ], (0, 2, 1)), B_GC, axis=-1)
    return coef, b8


def kernel(x_prompt, x_sample, state_conv, state_pool, g_pre, w_in, conv_w, conv_b, conv_ln_g, conv_ln_b,
           w_br_a, gmlp_ln_g, gmlp_ln_b, gmlp_ws, gmlp_bs, w_br_b, pool_w, pool_scale, w_br_c, w_out, g_post):
    depth = w_in.shape[0]
    n_seq, seq_len, _ = x_prompt.shape
    n_dec, dec_seq, _ = x_sample.shape
    assert dec_seq == DEC_SEQ and x_prompt.shape[2] == D_MODEL

    wcat = jnp.concatenate(
        [w_in[:, :, b * COL:(b + 1) * COL] for b in _W_IN_BLOCK_ORDER] + [w_out], axis=2).astype(bf16)
    w2 = jnp.stack([w[:, :, h * COL:(h + 1) * COL] for w in (w_br_a, w_br_b, w_br_c) for h in (0, 1)],
                   axis=1).astype(bf16)
    vd = jnp.stack([g_pre, g_post], axis=1)
    ve = jnp.stack([conv_b, conv_ln_g, conv_ln_b, gmlp_ln_g, gmlp_ln_b, pool_scale], axis=1)
    bs_full = jnp.repeat(jnp.transpose(gmlp_bs, (0, 2, 1)), B_GC, axis=-1)
    pw = pool_w.astype(bf16)
    coef, b8 = _sample_mix_tables(gmlp_ws, gmlp_bs)

    xp = x_prompt.reshape(n_seq * seq_len, D_MODEL)
    xs = x_sample.reshape(n_dec * DEC_SEQ, D_MODEL)
    conv_p, pool_p, z_s, c_s, v_s = [], [], [], [], []
    for l in range(depth):
        xp, cst, pst = _prompt_layer(l, xp, n_seq, seq_len, wcat, w2, vd, ve, conv_w, gmlp_ws, bs_full, pw)
        xs, z_new, c_new, v_new = _sample_layer(l, xs, wcat, w2, vd, ve, conv_w, coef, b8, pw,
                                                state_conv, state_pool)
        conv_p.append(cst); pool_p.append(pst)
        z_s.append(z_new); c_s.append(c_new); v_s.append(v_new)

    new_conv_sample = jnp.concatenate([state_conv[:, :, DEC_SEQ:], jnp.stack(z_s)], axis=2)
    new_pool_sample = jnp.concatenate([state_pool[:, :, DEC_SEQ:], jnp.stack(c_s)], axis=2)
    return (xp.reshape(n_seq, seq_len, D_MODEL), xs.reshape(n_dec, DEC_SEQ, D_MODEL),
            jnp.stack(conv_p), jnp.stack(pool_p), new_conv_sample, new_pool_sample, jnp.stack(v_s))
```

```python
import functools

import jax
import jax.numpy as jnp
import numpy as np
from jax import lax
from jax.experimental import pallas as pl
from jax.experimental.pallas import tpu as pltpu

D_MODEL = 2048
E = 1024
N_BRANCH = 3
CONV_W = 31
CONV_HALO = 32
SUBLANES = 8
LANES = 128
CHUNK = 128
B_GROUPS = 8
B_GC = E // B_GROUPS
POOL_WINDOWS = (2, 4, 8, 16)
C_GC = E // len(POOL_WINDOWS)
POOL_MAX = 16
POOL_HALO = 16
DEC_SEQ = 8
PAST_LEN = 16384
EPS = 1e-6
COL = 1024
N_IN_BLOCKS = (8 * E + N_BRANCH * D_MODEL) // COL
N_STEPS = N_IN_BLOCKS + D_MODEL // COL
FIRST_MERGE_STEP = 8

TM_PROMPT = 512
TM_SAMPLE = 256
VMEM_LIMIT_BYTES = 58 * 1024 * 1024

f32 = jnp.float32
bf16 = jnp.bfloat16


def _for_rows(n_rows, rc, fn):
    def body(c, carry):
        fn(pl.multiple_of(c * rc, rc))
        return carry
    lax.fori_loop(0, n_rows // rc, body, 0)


def _sigmoid(x):
    return 0.5 * jnp.tanh(0.5 * x) + 0.5


def _layernorm_rows(y, g, b):
    mu = jnp.mean(y, axis=-1, keepdims=True)
    yc = y - mu
    var = jnp.mean(yc * yc, axis=-1, keepdims=True)
    return yc * lax.rsqrt(var + EPS) * g + b


def _pre_norm(x_ref, vd_ref, lhs_ref, tm):
    def go(r0):
        r = pl.ds(r0, 128)
        x = x_ref[r, :]
        ms = jnp.mean(x * x, axis=-1, keepdims=True)
        lhs_ref[r, :] = (x * lax.rsqrt(ms + EPS) * vd_ref[0:1, :]).astype(bf16)
    _for_rows(tm, 128, go)


def _widen(p_ref, s1_ref, su_ref, tm):
    def go(r0):
        r = pl.ds(r0, 64)
        p = p_ref[r, :]
        su_ref[r, :] = (s1_ref[r, :] * (p * _sigmoid(p))).astype(bf16)
    _for_rows(tm, 64, go)


def _merge(p_ref, su_ref, w2_ref, m_ref, half, first):
    cols = slice(half * COL, (half + 1) * COL)
    y = jnp.dot(su_ref[...], w2_ref[...], preferred_element_type=f32)
    g = _sigmoid(p_ref[...]) * y
    if first:
        m_ref[:, cols] = g
    else:
        m_ref[:, cols] += g


def _m_to_lhs(m_ref, lhs_ref, tm):
    def go(r0):
        r = pl.ds(r0, 64)
        lhs_ref[r, :] = m_ref[r, :].astype(bf16)
    _for_rows(tm, 64, go)


def _post_norm_residual(x_ref, y0_ref, y1_ref, vd_ref, out_ref, tm):
    def go(r0):
        r = pl.ds(r0, 128)
        y0 = y0_ref[r, :]
        y1 = y1_ref[r, :]
        ss = jnp.sum(y0 * y0, axis=-1, keepdims=True) + jnp.sum(y1 * y1, axis=-1, keepdims=True)
        inv = lax.rsqrt(ss / D_MODEL + EPS)
        out_ref[r, 0:COL] = x_ref[r, 0:COL] + y0 * inv * vd_ref[1:2, 0:COL]
        out_ref[r, COL:D_MODEL] = x_ref[r, COL:D_MODEL] + y1 * inv * vd_ref[1:2, COL:D_MODEL]
    _for_rows(tm, 128, go)


def _ln_rows(src_ref, dst_ref, g, b, tm, swish):
    rc = min(tm, 256)
    def go(r0):
        r = pl.ds(r0, rc)
        y = _layernorm_rows(src_ref[r, :], g, b)
        dst_ref[r, :] = y * _sigmoid(y) if swish else y
    _for_rows(tm, rc, go)


def _project(s, lhs_ref, w_ref, p_ref):
    p_ref[s % 2] = jnp.dot(lhs_ref[...], w_ref[...], preferred_element_type=f32)


def _common_steps(s, x_ref, w2_ref, vd_ref, lhs_ref, p_ref, s1_ref, su_ref, m_ref, out_ref, tm):
    for step, branch in ((2, 0), (5, 1), (7, 2)):
        @pl.when(s == step)
        def _(step=step, branch=branch):
            _widen(p_ref.at[step % 2], s1_ref, su_ref.at[branch], tm)

    for k in range(2 * N_BRANCH):
        step = FIRST_MERGE_STEP + k
        @pl.when(s == step)
        def _(step=step, k=k):
            _merge(p_ref.at[step % 2], su_ref.at[k // 2], w2_ref, m_ref, k % 2, first=k < 2)

    @pl.when(s == N_IN_BLOCKS - 1)
    def _():
        _m_to_lhs(m_ref, lhs_ref, tm)

    @pl.when(s == N_STEPS - 1)
    def _():
        _post_norm_residual(x_ref, p_ref.at[0], p_ref.at[1], vd_ref, out_ref, tm)


def _conv_rows_prompt(zbuf_ref, cw_ref, ve_ref, dst_ref, r0, rc):
    for cb in range(E // LANES):
        cols = slice(cb * LANES, (cb + 1) * LANES)
        blk = zbuf_ref[pl.ds(r0, rc + CONV_HALO), cols]
        lead = CONV_HALO - (CONV_W - 1)
        y = ve_ref[0:1, cols] + cw_ref[CONV_W - 1:CONV_W, cols] * blk[CONV_HALO:CONV_HALO + rc, :]
        for b in range(SUBLANES):
            acc = None
            for a in range(CONV_HALO // SUBLANES):
                k = SUBLANES * a + b - lead
                if k < 0:
                    continue
                term = cw_ref[k:k + 1, cols] * blk[SUBLANES * a:SUBLANES * a + rc + SUBLANES, :]
                acc = term if acc is None else acc + term
            y = y + acc[b:b + rc, :]
        dst_ref[pl.ds(r0, rc), cols] = y


def _prompt_kernel(x_ref, w_ref, w2_ref, vd_ref, ve_ref, cw_ref, ws_ref, bs_ref, pw_ref,
                   out_ref, cst_ref, pst_ref,
                   lhs_ref, p_ref, s1_ref, su_ref, m_ref, zbuf_ref, cbuf_ref,
                   *, tiles_per_seq):
    tm = TM_PROMPT
    i = pl.program_id(0)
    s = pl.program_id(1)
    q = i % tiles_per_seq

    @pl.when(s == 0)
    def _():
        _pre_norm(x_ref, vd_ref, lhs_ref, tm)

    _project(s, lhs_ref, w_ref, p_ref)

    _common_steps(s, x_ref, w2_ref, vd_ref, lhs_ref, p_ref, s1_ref, su_ref, m_ref, out_ref, tm)

    @pl.when(s == 1)
    def _():
        @pl.when(q == 0)
        def _():
            zbuf_ref[0:CONV_HALO, :] = jnp.zeros((CONV_HALO, E), f32)

        def glu(r0):
            r = pl.ds(r0, 64)
            zbuf_ref[pl.ds(CONV_HALO + r0, 64), :] = p_ref[0, r, :] * _sigmoid(p_ref[1, r, :])
        _for_rows(tm, 64, glu)

        _for_rows(tm, CHUNK, lambda r0: _conv_rows_prompt(zbuf_ref, cw_ref, ve_ref, s1_ref, r0, CHUNK))
        _ln_rows(s1_ref, s1_ref, ve_ref[1:2, :], ve_ref[2:3, :], tm, swish=True)

        cst_ref[...] = zbuf_ref[CONV_HALO + tm - (CONV_W - 1):CONV_HALO + tm, :]
        zbuf_ref[0:CONV_HALO, :] = zbuf_ref[tm:tm + CONV_HALO, :]

    @pl.when(s == 4)
    def _():
        _ln_rows(p_ref.at[0], s1_ref, ve_ref[3:4, :], ve_ref[4:5, :], tm, swish=False)
        tril = (lax.broadcasted_iota(jnp.int32, (CHUNK, CHUNK), 0)
                >= lax.broadcasted_iota(jnp.int32, (CHUNK, CHUNK), 1))
        for g in range(B_GROUPS):
            cols = slice(g * B_GC, (g + 1) * B_GC)
            w_g = jnp.where(tril, ws_ref[g], 0.0).astype(bf16)
            for c in range(tm // CHUNK):
                rows = slice(c * CHUNK, (c + 1) * CHUNK)
                v = s1_ref[rows, cols].astype(bf16)
                mixed = jnp.dot(w_g, v, preferred_element_type=f32) + bs_ref[:, cols]
                s1_ref[rows, cols] = p_ref[1, rows, cols] * mixed

    @pl.when(s == 6)
    def _():
        @pl.when(q == 0)
        def _():
            cbuf_ref[0:POOL_HALO, :] = jnp.zeros((POOL_HALO, E), f32)

        def stash(r0):
            cbuf_ref[pl.ds(POOL_HALO + r0, 64), :] = p_ref[0, pl.ds(r0, 64), :]
        _for_rows(tm, 64, stash)

        def pool(r0):
            pos = q * tm + r0 + lax.broadcasted_iota(jnp.int32, (CHUNK, C_GC), 0)
            for gi, w in enumerate(POOL_WINDOWS):
                cols = slice(gi * C_GC, (gi + 1) * C_GC)
                blk = cbuf_ref[pl.ds(r0, POOL_HALO + CHUNK), cols]
                c = blk[POOL_HALO:, :]
                win = c
                for j in range(1, w):
                    win = win + blk[POOL_HALO - j:POOL_HALO - j + CHUNK, :]
                cnt = jnp.minimum(w, pos + 1).astype(f32)
                d = win / cnt - c
                y = jnp.dot(d.astype(bf16), pw_ref[gi], preferred_element_type=f32)
                s1_ref[pl.ds(r0, CHUNK), cols] = y * ve_ref[5:6, cols]
        _for_rows(tm, CHUNK, pool)

        pst_ref[...] = cbuf_ref[POOL_HALO + tm - (POOL_MAX - 1):POOL_HALO + tm, :]
        cbuf_ref[0:POOL_HALO, :] = cbuf_ref[tm:tm + POOL_HALO, :]


def _sample_kernel(x_ref, w_ref, w2_ref, vd_ref, ve_ref, cw_ref, coef_ref, b8_ref, pw_ref,
                   stc_ref, stp_ref,
                   out_ref, z_ref, c_ref, v_ref,
                   lhs_ref, p_ref, s1_ref, su_ref, m_ref, tail_ref):
    tm = TM_SAMPLE
    n_seq = tm // DEC_SEQ
    sb = 8
    s = pl.program_id(1)

    @pl.when(s == 0)
    def _():
        _pre_norm(x_ref, vd_ref, lhs_ref, tm)

    _project(s, lhs_ref, w_ref, p_ref)

    _common_steps(s, x_ref, w2_ref, vd_ref, lhs_ref, p_ref, s1_ref, su_ref, m_ref, out_ref, tm)

    def rows_of(q0):
        return pl.ds(pl.multiple_of(q0 * DEC_SEQ, sb * DEC_SEQ), sb * DEC_SEQ)

    @pl.when(s == 1)
    def _():
        def go(q0):
            qs = pl.ds(q0, sb)
            r = rows_of(q0)
            z = (p_ref[0, r, :] * _sigmoid(p_ref[1, r, :])).reshape(sb, DEC_SEQ, E)
            z_ref[qs, :, :] = z
            tail_ref[qs, 0:8, :] = stc_ref[qs, CONV_W - 1 - 8:CONV_W - 1, :]
            tail_ref[qs, 8:16, :] = z
            for cb in range(E // 256):
                cols = slice(cb * 256, (cb + 1) * 256)
                acc = jnp.broadcast_to(ve_ref[0:1, cols].reshape(1, 1, 256), (sb, DEC_SEQ, 256))
                for k in range(CONV_W):
                    wk = cw_ref[k:k + 1, cols].reshape(1, 1, 256)
                    if k + DEC_SEQ <= CONV_W - 1:
                        win = stc_ref[qs, k:k + DEC_SEQ, cols]
                    else:
                        o = k - (CONV_W - 1 - 8)
                        win = tail_ref[qs, o:o + DEC_SEQ, cols]
                    acc = acc + wk * win
                s1_ref[r, cols] = acc.reshape(sb * DEC_SEQ, 256)
        _for_rows(n_seq, sb, go)
        _ln_rows(s1_ref, s1_ref, ve_ref[1:2, :], ve_ref[2:3, :], tm, swish=True)

    @pl.when(s == 4)
    def _():
        _ln_rows(p_ref.at[0], s1_ref, ve_ref[3:4, :], ve_ref[4:5, :], tm, swish=False)

        def go(q0):
            qs = pl.ds(q0, sb)
            r = rows_of(q0)
            v = s1_ref[r, :].reshape(sb, DEC_SEQ, E)
            v_ref[qs, :, :] = v
            tail_ref[qs, 0:8, :] = jnp.zeros((sb, 8, E), f32)
            tail_ref[qs, 8:16, :] = v
            for cb in range(E // 256):
                cols = slice(cb * 256, (cb + 1) * 256)
                acc = jnp.broadcast_to(b8_ref[:, cols].reshape(1, DEC_SEQ, 256), (sb, DEC_SEQ, 256))
                for j in range(DEC_SEQ):
                    acc = acc + coef_ref[j, :, cols].reshape(1, DEC_SEQ, 256) * tail_ref[qs, 8 - j:16 - j, cols]
                s1_ref[r, cols] = p_ref[1, r, cols] * acc.reshape(sb * DEC_SEQ, 256)
        _for_rows(n_seq, sb, go)

    @pl.when(s == 6)
    def _():
        def go(q0):
            qs = pl.ds(q0, sb)
            r = rows_of(q0)
            c3 = p_ref[0, r, :].reshape(sb, DEC_SEQ, E)
            c_ref[qs, :, :] = c3
            tail_ref[qs, 0:8, :] = stp_ref[qs, POOL_MAX - 1 - 8:POOL_MAX - 1, :]
            tail_ref[qs, 8:16, :] = c3
            for gi, w in enumerate(POOL_WINDOWS):
                cols = slice(gi * C_GC, (gi + 1) * C_GC)
                c = tail_ref[qs, 8:16, cols]
                win = c
                for j in range(1, w):
                    if j < 8:
                        win = win + tail_ref[qs, 8 - j:16 - j, cols]
                    else:
                        win = win + stp_ref[qs, POOL_MAX - 1 - j:POOL_MAX - 1 - j + DEC_SEQ, cols]
                d = win / float(min(w, PAST_LEN + 1)) - c
                s1_ref[r, cols] = d.reshape(sb * DEC_SEQ, C_GC)
        _for_rows(n_seq, sb, go)

        for gi in range(len(POOL_WINDOWS)):
            cols = slice(gi * C_GC, (gi + 1) * C_GC)
            y = jnp.dot(s1_ref[:, cols].astype(bf16), pw_ref[gi], preferred_element_type=f32)
            s1_ref[:, cols] = y * ve_ref[5:6, cols]


def _w2_index(s):
    return jnp.clip(s - FIRST_MERGE_STEP, 0, 2 * N_BRANCH - 1)


def _weight_specs(l):
    return [
        pl.BlockSpec((None, None, D_MODEL, COL), lambda i, s: (l, s, 0, 0)),
        pl.BlockSpec((None, None, E, COL), lambda i, s: (l, _w2_index(s), 0, 0)),
        pl.BlockSpec((None, 2, D_MODEL), lambda i, s: (l, 0, 0)),
        pl.BlockSpec((None, 6, E), lambda i, s: (l, 0, 0)),
        pl.BlockSpec((None, CONV_W, E), lambda i, s: (l, 0, 0)),
    ]


def _work_buffers(tm):
    return [
        pltpu.VMEM((tm, D_MODEL), bf16),
        pltpu.VMEM((2, tm, COL), f32),
        pltpu.VMEM((tm, E), f32),
        pltpu.VMEM((N_BRANCH, tm, E), bf16),
        pltpu.VMEM((tm, D_MODEL), f32),
    ]


def _compiler_params():
    return pltpu.CompilerParams(dimension_semantics=("arbitrary", "arbitrary"),
                                vmem_limit_bytes=VMEM_LIMIT_BYTES)


def _prompt_layer(l, x2d, n_seq, seq_len, wblk, w2, vd, ve, cw, ws, bs_full, pw):
    tm = TM_PROMPT
    assert seq_len % tm == 0 and tm % CHUNK == 0
    tiles_per_seq = seq_len // tm
    n_rows = n_seq * seq_len
    grid = (n_rows // tm, N_STEPS)
    in_specs = [pl.BlockSpec((tm, D_MODEL), lambda i, s: (i, 0))] + _weight_specs(l) + [
        pl.BlockSpec((None, B_GROUPS, CHUNK, CHUNK), lambda i, s: (l, 0, 0, 0)),
        pl.BlockSpec((None, CHUNK, E), lambda i, s: (l, 0, 0)),
        pl.BlockSpec((None, len(POOL_WINDOWS), C_GC, C_GC), lambda i, s: (l, 0, 0, 0)),
    ]
    out_specs = [
        pl.BlockSpec((tm, D_MODEL), lambda i, s: (i, 0)),
        pl.BlockSpec((None, CONV_W - 1, E), lambda i, s: (i // tiles_per_seq, 0, 0)),
        pl.BlockSpec((None, POOL_MAX - 1, E), lambda i, s: (i // tiles_per_seq, 0, 0)),
    ]
    out_shape = [
        jax.ShapeDtypeStruct((n_rows, D_MODEL), f32),
        jax.ShapeDtypeStruct((n_seq, CONV_W - 1, E), f32),
        jax.ShapeDtypeStruct((n_seq, POOL_MAX - 1, E), f32),
    ]
    scratch = _work_buffers(tm) + [
        pltpu.VMEM((CONV_HALO + tm, E), f32),
        pltpu.VMEM((POOL_HALO + tm, E), f32),
    ]
    return pl.pallas_call(
        functools.partial(_prompt_kernel, tiles_per_seq=tiles_per_seq),
        grid=grid, in_specs=in_specs, out_specs=out_specs, out_shape=out_shape,
        scratch_shapes=scratch, compiler_params=_compiler_params(),
        name=f"prompt_layer{l}",
    )(x2d, wblk, w2, vd, ve, cw, ws, bs_full, pw)


def _sample_layer(l, x2d, wblk, w2, vd, ve, cw, coef, b8, pw, state_conv, state_pool):
    tm = TM_SAMPLE
    n_rows = x2d.shape[0]
    n_seq_total = n_rows // DEC_SEQ
    sq = tm // DEC_SEQ
    assert n_rows % tm == 0
    grid = (n_rows // tm, N_STEPS)
    in_specs = [pl.BlockSpec((tm, D_MODEL), lambda i, s: (i, 0))] + _weight_specs(l) + [
        pl.BlockSpec((None, DEC_SEQ, DEC_SEQ, E), lambda i, s: (l, 0, 0, 0)),
        pl.BlockSpec((None, DEC_SEQ, E), lambda i, s: (l, 0, 0)),
        pl.BlockSpec((None, len(POOL_WINDOWS), C_GC, C_GC), lambda i, s: (l, 0, 0, 0)),
        pl.BlockSpec((None, sq, CONV_W - 1, E), lambda i, s: (l, i, 0, 0)),
        pl.BlockSpec((None, sq, POOL_MAX - 1, E), lambda i, s: (l, i, 0, 0)),
    ]
    out_specs = [
        pl.BlockSpec((tm, D_MODEL), lambda i, s: (i, 0)),
        pl.BlockSpec((sq, DEC_SEQ, E), lambda i, s: (i, 0, 0)),
        pl.BlockSpec((sq, DEC_SEQ, E), lambda i, s: (i, 0, 0)),
        pl.BlockSpec((sq, DEC_SEQ, E), lambda i, s: (i, 0, 0)),
    ]
    out_shape = [
        jax.ShapeDtypeStruct((n_rows, D_MODEL), f32),
        jax.ShapeDtypeStruct((n_seq_total, DEC_SEQ, E), f32),
        jax.ShapeDtypeStruct((n_seq_total, DEC_SEQ, E), f32),
        jax.ShapeDtypeStruct((n_seq_total, DEC_SEQ, E), f32),
    ]
    scratch = _work_buffers(tm) + [
        pltpu.VMEM((sq, 16, E), f32),
    ]
    return pl.pallas_call(
        _sample_kernel,
        grid=grid, in_specs=in_specs, out_specs=out_specs, out_shape=out_shape,
        scratch_shapes=scratch, compiler_params=_compiler_params(),
        name=f"sample_layer{l}",
    )(x2d, wblk, w2, vd, ve, cw, coef, b8, pw, state_conv, state_pool)


def _column_blocks(w):
    depth, k, n = w.shape
    return jnp.transpose(w.reshape(depth, k, n // COL, COL), (0, 2, 1, 3))


def _sample_mix_tables(gmlp_ws, gmlp_bs):
    t = np.arange(DEC_SEQ)
    pick = (t[None, None, :] == t[None, :, None] - t[:, None, None]).astype(np.float32)
    w = jnp.sum(gmlp_ws[:, :, None, :DEC_SEQ, :DEC_SEQ] * pick[None, None], axis=-1)
    coef = jnp.repeat(jnp.transpose(w, (0, 2, 3, 1)), B_GC, axis=-1)
    b8 = jnp.repeat(jnp.transpose(gmlp_bs[:, :, :DEC_SEQ], (0, 2, 1)), B_GC, axis=-1)
    return coef, b8


def kernel(x_prompt, x_sample, state_conv, state_pool, g_pre, w_in, conv_w, conv_b, conv_ln_g, conv_ln_b,
           w_br_a, gmlp_ln_g, gmlp_ln_b, gmlp_ws, gmlp_bs, w_br_b, pool_w, pool_scale, w_br_c, w_out, g_post):
    depth = w_in.shape[0]
    n_seq, seq_len, _ = x_prompt.shape
    n_dec, dec_seq, _ = x_sample.shape
    assert dec_seq == DEC_SEQ and x_prompt.shape[2] == D_MODEL

    wblk = jnp.concatenate([_column_blocks(w_in), _column_blocks(w_out)], axis=1).astype(bf16)
    w2 = jnp.concatenate([_column_blocks(w) for w in (w_br_a, w_br_b, w_br_c)], axis=1).astype(bf16)
    vd = jnp.stack([g_pre, g_post], axis=1)
    ve = jnp.stack([conv_b, conv_ln_g, conv_ln_b, gmlp_ln_g, gmlp_ln_b, pool_scale], axis=1)
    bs_full = jnp.repeat(jnp.transpose(gmlp_bs, (0, 2, 1)), B_GC, axis=-1)
    pw = pool_w.astype(bf16)
    coef, b8 = _sample_mix_tables(gmlp_ws, gmlp_bs)

    xp = x_prompt.reshape(n_seq * seq_len, D_MODEL)
    xs = x_sample.reshape(n_dec * DEC_SEQ, D_MODEL)
    conv_p, pool_p, z_s, c_s, v_s = [], [], [], [], []
    for l in range(depth):
        xp, cst, pst = _prompt_layer(l, xp, n_seq, seq_len, wblk, w2, vd, ve, conv_w, gmlp_ws, bs_full, pw)
        xs, z_new, c_new, v_new = _sample_layer(l, xs, wblk, w2, vd, ve, conv_w, coef, b8, pw,
                                                state_conv, state_pool)
        conv_p.append(cst); pool_p.append(pst)
        z_s.append(z_new); c_s.append(c_new); v_s.append(v_new)

    new_conv_sample = jnp.concatenate([state_conv[:, :, DEC_SEQ:], jnp.stack(z_s)], axis=2)
    new_pool_sample = jnp.concatenate([state_pool[:, :, DEC_SEQ:], jnp.stack(c_s)], axis=2)
    return (xp.reshape(n_seq, seq_len, D_MODEL), xs.reshape(n_dec, DEC_SEQ, D_MODEL),
            jnp.stack(conv_p), jnp.stack(pool_p), new_conv_sample, new_pool_sample, jnp.stack(v_s))
```

```python
import functools

import jax
import jax.numpy as jnp
import numpy as np
from jax import lax
from jax.experimental import pallas as pl
from jax.experimental.pallas import tpu as pltpu

D_MODEL = 2048
E = 1024
N_BRANCH = 3
CONV_W = 31
CONV_HALO = 32
SUBLANES = 8
LANES = 128
CHUNK = 128
B_GROUPS = 8
B_GC = E // B_GROUPS
POOL_WINDOWS = (2, 4, 8, 16)
C_GC = E // len(POOL_WINDOWS)
POOL_MAX = 16
POOL_HALO = 16
DEC_SEQ = 8
PAST_LEN = 16384
EPS = 1e-6
COL = 1024
N_IN_BLOCKS = (8 * E + N_BRANCH * D_MODEL) // COL
N_STEPS = N_IN_BLOCKS + D_MODEL // COL
FIRST_MERGE_STEP = 8

TM_PROMPT = 512
TM_SAMPLE = 256
VMEM_LIMIT_BYTES = 60 * 1024 * 1024

f32 = jnp.float32
bf16 = jnp.bfloat16


def _for_rows(n_rows, rc, fn):
    def body(c, carry):
        fn(pl.multiple_of(c * rc, rc))
        return carry
    lax.fori_loop(0, n_rows // rc, body, 0)


def _sigmoid(x):
    return 0.5 * jnp.tanh(0.5 * x) + 0.5


def _layernorm_rows(y, g, b):
    mu = jnp.mean(y, axis=-1, keepdims=True)
    yc = y - mu
    var = jnp.mean(yc * yc, axis=-1, keepdims=True)
    return yc * lax.rsqrt(var + EPS) * g + b


def _start_layer(l, s, x_ref, vd_ref, out_ref, lhs_ref, tm):
    @pl.when(s == 0)
    def _():
        @pl.when(l == 0)
        def _():
            def load(r0):
                out_ref[pl.ds(r0, 128), :] = x_ref[pl.ds(r0, 128), :]
            _for_rows(tm, 128, load)

        def norm(r0):
            r = pl.ds(r0, 128)
            x = out_ref[r, :]
            ms = jnp.mean(x * x, axis=-1, keepdims=True)
            lhs_ref[r, :] = (x * lax.rsqrt(ms + EPS) * vd_ref[0:1, :]).astype(bf16)
        _for_rows(tm, 128, norm)


def _widen(p_ref, s1_ref, su_ref, tm):
    def go(r0):
        r = pl.ds(r0, 64)
        p = p_ref[r, :]
        su_ref[r, :] = (s1_ref[r, :] * (p * _sigmoid(p))).astype(bf16)
    _for_rows(tm, 64, go)


def _merge(p_ref, su_ref, w2_ref, m_ref, half, first):
    cols = slice(half * COL, (half + 1) * COL)
    y = jnp.dot(su_ref[...], w2_ref[...], preferred_element_type=f32)
    g = _sigmoid(p_ref[...]) * y
    if first:
        m_ref[:, cols] = g
    else:
        m_ref[:, cols] += g


def _m_to_lhs(m_ref, lhs_ref, tm):
    def go(r0):
        r = pl.ds(r0, 64)
        lhs_ref[r, :] = m_ref[r, :].astype(bf16)
    _for_rows(tm, 64, go)


def _post_norm_residual(y0_ref, y1_ref, vd_ref, x_ref, tm):
    def go(r0):
        r = pl.ds(r0, 128)
        y0 = y0_ref[r, :]
        y1 = y1_ref[r, :]
        ss = jnp.sum(y0 * y0, axis=-1, keepdims=True) + jnp.sum(y1 * y1, axis=-1, keepdims=True)
        inv = lax.rsqrt(ss / D_MODEL + EPS)
        x_ref[r, 0:COL] = x_ref[r, 0:COL] + y0 * inv * vd_ref[1:2, 0:COL]
        x_ref[r, COL:D_MODEL] = x_ref[r, COL:D_MODEL] + y1 * inv * vd_ref[1:2, COL:D_MODEL]
    _for_rows(tm, 128, go)


def _ln_rows(src_ref, dst_ref, g, b, tm, swish):
    rc = min(tm, 256)
    def go(r0):
        r = pl.ds(r0, rc)
        y = _layernorm_rows(src_ref[r, :], g, b)
        dst_ref[r, :] = y * _sigmoid(y) if swish else y
    _for_rows(tm, rc, go)


def _project(s, lhs_ref, w_ref, p_ref):
    p_ref[s % 2] = jnp.dot(lhs_ref[...], w_ref[...], preferred_element_type=f32)


def _common_steps(s, w2_ref, vd_ref, lhs_ref, p_ref, s1_ref, su_ref, m_ref, out_ref, tm):
    for step, branch in ((2, 0), (5, 1), (7, 2)):
        @pl.when(s == step)
        def _(step=step, branch=branch):
            _widen(p_ref.at[step % 2], s1_ref, su_ref.at[branch], tm)

    for k in range(2 * N_BRANCH):
        step = FIRST_MERGE_STEP + k
        @pl.when(s == step)
        def _(step=step, k=k):
            _merge(p_ref.at[step % 2], su_ref.at[k // 2], w2_ref, m_ref, k % 2, first=k < 2)

    @pl.when(s == N_IN_BLOCKS - 1)
    def _():
        _m_to_lhs(m_ref, lhs_ref, tm)

    @pl.when(s == N_STEPS - 1)
    def _():
        _post_norm_residual(p_ref.at[0], p_ref.at[1], vd_ref, out_ref, tm)


def _conv_rows_prompt(zbuf_ref, cw_ref, ve_ref, dst_ref, r0, rc):
    for cb in range(E // LANES):
        cols = slice(cb * LANES, (cb + 1) * LANES)
        blk = zbuf_ref[pl.ds(r0, rc + CONV_HALO), cols]
        lead = CONV_HALO - (CONV_W - 1)
        y = ve_ref[0:1, cols] + cw_ref[CONV_W - 1:CONV_W, cols] * blk[CONV_HALO:CONV_HALO + rc, :]
        for b in range(SUBLANES):
            acc = None
            for a in range(CONV_HALO // SUBLANES):
                k = SUBLANES * a + b - lead
                if k < 0:
                    continue
                term = cw_ref[k:k + 1, cols] * blk[SUBLANES * a:SUBLANES * a + rc + SUBLANES, :]
                acc = term if acc is None else acc + term
            y = y + acc[b:b + rc, :]
        dst_ref[pl.ds(r0, rc), cols] = y


def _prompt_kernel(x_ref, w_ref, w2_ref, vd_ref, ve_ref, cw_ref, ws_ref, bs_ref, pw_ref,
                   out_ref, cst_ref, pst_ref,
                   lhs_ref, p_ref, s1_ref, su_ref, m_ref, zbuf_ref, cbuf_ref, zhalo_ref, chalo_ref,
                   *, tiles_per_seq):
    tm = TM_PROMPT
    i = pl.program_id(0)
    l = pl.program_id(1)
    s = pl.program_id(2)
    q = i % tiles_per_seq

    _start_layer(l, s, x_ref, vd_ref, out_ref, lhs_ref, tm)
    _project(s, lhs_ref, w_ref, p_ref)
    _common_steps(s, w2_ref, vd_ref, lhs_ref, p_ref, s1_ref, su_ref, m_ref, out_ref, tm)

    @pl.when(s == 1)
    def _():
        @pl.when(q == 0)
        def _():
            zbuf_ref[0:CONV_HALO, :] = jnp.zeros((CONV_HALO, E), f32)

        @pl.when(q > 0)
        def _():
            zbuf_ref[0:CONV_HALO, :] = zhalo_ref[l]

        def glu(r0):
            r = pl.ds(r0, 64)
            zbuf_ref[pl.ds(CONV_HALO + r0, 64), :] = p_ref[0, r, :] * _sigmoid(p_ref[1, r, :])
        _for_rows(tm, 64, glu)

        _for_rows(tm, CHUNK, lambda r0: _conv_rows_prompt(zbuf_ref, cw_ref, ve_ref, s1_ref, r0, CHUNK))
        _ln_rows(s1_ref, s1_ref, ve_ref[1:2, :], ve_ref[2:3, :], tm, swish=True)

        cst_ref[...] = zbuf_ref[CONV_HALO + tm - (CONV_W - 1):CONV_HALO + tm, :]
        zhalo_ref[l] = zbuf_ref[tm:tm + CONV_HALO, :]

    @pl.when(s == 4)
    def _():
        _ln_rows(p_ref.at[0], s1_ref, ve_ref[3:4, :], ve_ref[4:5, :], tm, swish=False)
        tril = (lax.broadcasted_iota(jnp.int32, (CHUNK, CHUNK), 0)
                >= lax.broadcasted_iota(jnp.int32, (CHUNK, CHUNK), 1))
        for g in range(B_GROUPS):
            cols = slice(g * B_GC, (g + 1) * B_GC)
            w_g = jnp.where(tril, ws_ref[g], 0.0).astype(bf16)
            for c in range(tm // CHUNK):
                rows = slice(c * CHUNK, (c + 1) * CHUNK)
                v = s1_ref[rows, cols].astype(bf16)
                mixed = jnp.dot(w_g, v, preferred_element_type=f32) + bs_ref[:, cols]
                s1_ref[rows, cols] = p_ref[1, rows, cols] * mixed

    @pl.when(s == 6)
    def _():
        @pl.when(q == 0)
        def _():
            cbuf_ref[0:POOL_HALO, :] = jnp.zeros((POOL_HALO, E), f32)

        @pl.when(q > 0)
        def _():
            cbuf_ref[0:POOL_HALO, :] = chalo_ref[l]

        def stash(r0):
            cbuf_ref[pl.ds(POOL_HALO + r0, 64), :] = p_ref[0, pl.ds(r0, 64), :]
        _for_rows(tm, 64, stash)

        def pool(r0):
            pos = q * tm + r0 + lax.broadcasted_iota(jnp.int32, (CHUNK, C_GC), 0)
            for gi, w in enumerate(POOL_WINDOWS):
                cols = slice(gi * C_GC, (gi + 1) * C_GC)
                blk = cbuf_ref[pl.ds(r0, POOL_HALO + CHUNK), cols]
                c = blk[POOL_HALO:, :]
                win = c
                for j in range(1, w):
                    win = win + blk[POOL_HALO - j:POOL_HALO - j + CHUNK, :]
                cnt = jnp.minimum(w, pos + 1).astype(f32)
                d = win / cnt - c
                y = jnp.dot(d.astype(bf16), pw_ref[gi], preferred_element_type=f32)
                s1_ref[pl.ds(r0, CHUNK), cols] = y * ve_ref[5:6, cols]
        _for_rows(tm, CHUNK, pool)

        pst_ref[...] = cbuf_ref[POOL_HALO + tm - (POOL_MAX - 1):POOL_HALO + tm, :]
        chalo_ref[l] = cbuf_ref[tm:tm + POOL_HALO, :]


def _sample_kernel(x_ref, w_ref, w2_ref, vd_ref, ve_ref, cw_ref, coef_ref, b8_ref, pw_ref,
                   stc_ref, stp_ref,
                   out_ref, cso_ref, pso_ref, v_ref,
                   lhs_ref, p_ref, s1_ref, su_ref, m_ref, tail_ref):
    tm = TM_SAMPLE
    n_seq = tm // DEC_SEQ
    sb = 8
    l = pl.program_id(1)
    s = pl.program_id(2)
    conv_keep = CONV_W - 1 - DEC_SEQ
    pool_keep = POOL_MAX - 1 - DEC_SEQ

    _start_layer(l, s, x_ref, vd_ref, out_ref, lhs_ref, tm)
    _project(s, lhs_ref, w_ref, p_ref)
    _common_steps(s, w2_ref, vd_ref, lhs_ref, p_ref, s1_ref, su_ref, m_ref, out_ref, tm)

    def rows_of(q0):
        return pl.ds(pl.multiple_of(q0 * DEC_SEQ, sb * DEC_SEQ), sb * DEC_SEQ)

    @pl.when(s == 1)
    def _():
        def go(q0):
            qs = pl.ds(q0, sb)
            r = rows_of(q0)
            z = (p_ref[0, r, :] * _sigmoid(p_ref[1, r, :])).reshape(sb, DEC_SEQ, E)
            cso_ref[qs, 0:conv_keep, :] = stc_ref[qs, DEC_SEQ:CONV_W - 1, :]
            cso_ref[qs, conv_keep:CONV_W - 1, :] = z
            tail_ref[qs, 0:8, :] = stc_ref[qs, conv_keep:CONV_W - 1, :]
            tail_ref[qs, 8:16, :] = z
            for cb in range(E // 256):
                cols = slice(cb * 256, (cb + 1) * 256)
                acc = jnp.broadcast_to(ve_ref[0:1, cols].reshape(1, 1, 256), (sb, DEC_SEQ, 256))
                for k in range(CONV_W):
                    wk = cw_ref[k:k + 1, cols].reshape(1, 1, 256)
                    if k + DEC_SEQ <= CONV_W - 1:
                        win = stc_ref[qs, k:k + DEC_SEQ, cols]
                    else:
                        o = k - conv_keep
                        win = tail_ref[qs, o:o + DEC_SEQ, cols]
                    acc = acc + wk * win
                s1_ref[r, cols] = acc.reshape(sb * DEC_SEQ, 256)
        _for_rows(n_seq, sb, go)
        _ln_rows(s1_ref, s1_ref, ve_ref[1:2, :], ve_ref[2:3, :], tm, swish=True)

    @pl.when(s == 4)
    def _():
        _ln_rows(p_ref.at[0], s1_ref, ve_ref[3:4, :], ve_ref[4:5, :], tm, swish=False)

        def go(q0):
            qs = pl.ds(q0, sb)
            r = rows_of(q0)
            v = s1_ref[r, :].reshape(sb, DEC_SEQ, E)
            v_ref[qs, :, :] = v
            tail_ref[qs, 0:8, :] = jnp.zeros((sb, 8, E), f32)
            tail_ref[qs, 8:16, :] = v
            for cb in range(E // 256):
                cols = slice(cb * 256, (cb + 1) * 256)
                acc = jnp.broadcast_to(b8_ref[:, cols].reshape(1, DEC_SEQ, 256), (sb, DEC_SEQ, 256))
                for j in range(DEC_SEQ):
                    acc = acc + coef_ref[j, :, cols].reshape(1, DEC_SEQ, 256) * tail_ref[qs, 8 - j:16 - j, cols]
                s1_ref[r, cols] = p_ref[1, r, cols] * acc.reshape(sb * DEC_SEQ, 256)
        _for_rows(n_seq, sb, go)

    @pl.when(s == 6)
    def _():
        def go(q0):
            qs = pl.ds(q0, sb)
            r = rows_of(q0)
            c3 = p_ref[0, r, :].reshape(sb, DEC_SEQ, E)
            pso_ref[qs, 0:pool_keep, :] = stp_ref[qs, DEC_SEQ:POOL_MAX - 1, :]
            pso_ref[qs, pool_keep:POOL_MAX - 1, :] = c3
            tail_ref[qs, 0:8, :] = stp_ref[qs, pool_keep:POOL_MAX - 1, :]
            tail_ref[qs, 8:16, :] = c3
            for gi, w in enumerate(POOL_WINDOWS):
                cols = slice(gi * C_GC, (gi + 1) * C_GC)
                c = tail_ref[qs, 8:16, cols]
                win = c
                for j in range(1, w):
                    if j < 8:
                        win = win + tail_ref[qs, 8 - j:16 - j, cols]
                    else:
                        win = win + stp_ref[qs, POOL_MAX - 1 - j:POOL_MAX - 1 - j + DEC_SEQ, cols]
                d = win / float(min(w, PAST_LEN + 1)) - c
                s1_ref[r, cols] = d.reshape(sb * DEC_SEQ, C_GC)
        _for_rows(n_seq, sb, go)

        for gi in range(len(POOL_WINDOWS)):
            cols = slice(gi * C_GC, (gi + 1) * C_GC)
            y = jnp.dot(s1_ref[:, cols].astype(bf16), pw_ref[gi], preferred_element_type=f32)
            s1_ref[:, cols] = y * ve_ref[5:6, cols]


def _w2_index(s):
    return jnp.clip(s - FIRST_MERGE_STEP, 0, 2 * N_BRANCH - 1)


def _weight_specs():
    return [
        pl.BlockSpec((None, D_MODEL, COL), lambda i, l, s: (l, 0, s)),
        pl.BlockSpec((None, E, COL), lambda i, l, s: (l, 0, _w2_index(s))),
        pl.BlockSpec((None, 2, D_MODEL), lambda i, l, s: (l, 0, 0)),
        pl.BlockSpec((None, 6, E), lambda i, l, s: (l, 0, 0)),
        pl.BlockSpec((None, CONV_W, E), lambda i, l, s: (l, 0, 0)),
    ]


def _work_buffers(tm):
    return [
        pltpu.VMEM((tm, D_MODEL), bf16),
        pltpu.VMEM((2, tm, COL), f32),
        pltpu.VMEM((tm, E), f32),
        pltpu.VMEM((N_BRANCH, tm, E), bf16),
        pltpu.VMEM((tm, D_MODEL), f32),
    ]


def _compiler_params():
    return pltpu.CompilerParams(dimension_semantics=("arbitrary", "arbitrary", "arbitrary"),
                                vmem_limit_bytes=VMEM_LIMIT_BYTES)


def _prompt_group(depth, x2d, n_seq, seq_len, wcat, w2cat, vd, ve, cw, ws, bs_full, pw):
    tm = TM_PROMPT
    assert seq_len % tm == 0 and tm % CHUNK == 0
    tiles_per_seq = seq_len // tm
    n_rows = n_seq * seq_len
    grid = (n_rows // tm, depth, N_STEPS)
    in_specs = [pl.BlockSpec((tm, D_MODEL), lambda i, l, s: (i, 0))] + _weight_specs() + [
        pl.BlockSpec((None, B_GROUPS, CHUNK, CHUNK), lambda i, l, s: (l, 0, 0, 0)),
        pl.BlockSpec((None, CHUNK, E), lambda i, l, s: (l, 0, 0)),
        pl.BlockSpec((None, len(POOL_WINDOWS), C_GC, C_GC), lambda i, l, s: (l, 0, 0, 0)),
    ]
    out_specs = [
        pl.BlockSpec((tm, D_MODEL), lambda i, l, s: (i, 0)),
        pl.BlockSpec((None, None, CONV_W - 1, E), lambda i, l, s: (l, i, 0, 0)),
        pl.BlockSpec((None, None, POOL_MAX - 1, E), lambda i, l, s: (l, i, 0, 0)),
    ]
    out_shape = [
        jax.ShapeDtypeStruct((n_rows, D_MODEL), f32),
        jax.ShapeDtypeStruct((depth, n_rows // tm, CONV_W - 1, E), f32),
        jax.ShapeDtypeStruct((depth, n_rows // tm, POOL_MAX - 1, E), f32),
    ]
    scratch = _work_buffers(tm) + [
        pltpu.VMEM((CONV_HALO + tm, E), f32),
        pltpu.VMEM((POOL_HALO + tm, E), f32),
        pltpu.VMEM((depth, CONV_HALO, E), f32),
        pltpu.VMEM((depth, POOL_HALO, E), f32),
    ]
    x_out, conv_tiles, pool_tiles = pl.pallas_call(
        functools.partial(_prompt_kernel, tiles_per_seq=tiles_per_seq),
        grid=grid, in_specs=in_specs, out_specs=out_specs, out_shape=out_shape,
        scratch_shapes=scratch, compiler_params=_compiler_params(),
        name="prompt_group",
    )(x2d, wcat, w2cat, vd, ve, cw, ws, bs_full, pw)
    last = slice(tiles_per_seq - 1, None, tiles_per_seq)
    return x_out, conv_tiles[:, last], pool_tiles[:, last]


def _sample_group(depth, x2d, wcat, w2cat, vd, ve, cw, coef, b8, pw, state_conv, state_pool):
    tm = TM_SAMPLE
    n_rows = x2d.shape[0]
    n_seq_total = n_rows // DEC_SEQ
    sq = tm // DEC_SEQ
    assert n_rows % tm == 0
    grid = (n_rows // tm, depth, N_STEPS)
    in_specs = [pl.BlockSpec((tm, D_MODEL), lambda i, l, s: (i, 0))] + _weight_specs() + [
        pl.BlockSpec((None, DEC_SEQ, DEC_SEQ, E), lambda i, l, s: (l, 0, 0, 0)),
        pl.BlockSpec((None, DEC_SEQ, E), lambda i, l, s: (l, 0, 0)),
        pl.BlockSpec((None, len(POOL_WINDOWS), C_GC, C_GC), lambda i, l, s: (l, 0, 0, 0)),
        pl.BlockSpec((None, sq, CONV_W - 1, E), lambda i, l, s: (l, i, 0, 0)),
        pl.BlockSpec((None, sq, POOL_MAX - 1, E), lambda i, l, s: (l, i, 0, 0)),
    ]
    out_specs = [
        pl.BlockSpec((tm, D_MODEL), lambda i, l, s: (i, 0)),
        pl.BlockSpec((None, sq, CONV_W - 1, E), lambda i, l, s: (l, i, 0, 0)),
        pl.BlockSpec((None, sq, POOL_MAX - 1, E), lambda i, l, s: (l, i, 0, 0)),
        pl.BlockSpec((None, sq, DEC_SEQ, E), lambda i, l, s: (l, i, 0, 0)),
    ]
    out_shape = [
        jax.ShapeDtypeStruct((n_rows, D_MODEL), f32),
        jax.ShapeDtypeStruct((depth, n_seq_total, CONV_W - 1, E), f32),
        jax.ShapeDtypeStruct((depth, n_seq_total, POOL_MAX - 1, E), f32),
        jax.ShapeDtypeStruct((depth, n_seq_total, DEC_SEQ, E), f32),
    ]
    scratch = _work_buffers(tm) + [
        pltpu.VMEM((sq, 16, E), f32),
    ]
    return pl.pallas_call(
        _sample_kernel,
        grid=grid, in_specs=in_specs, out_specs=out_specs, out_shape=out_shape,
        scratch_shapes=scratch, compiler_params=_compiler_params(),
        name="sample_group",
    )(x2d, wcat, w2cat, vd, ve, cw, coef, b8, pw, state_conv, state_pool)


def _sample_mix_tables(gmlp_ws, gmlp_bs):
    t = np.arange(DEC_SEQ)
    pick = (t[None, None, :] == t[None, :, None] - t[:, None, None]).astype(np.float32)
    w = jnp.sum(gmlp_ws[:, :, None, :DEC_SEQ, :DEC_SEQ] * pick[None, None], axis=-1)
    coef = jnp.repeat(jnp.transpose(w, (0, 2, 3, 1)), B_GC, axis=-1)
    b8 = jnp.repeat(jnp.transpose(gmlp_bs[:, :, :DEC_SEQ], (0, 2, 1)), B_GC, axis=-1)
    return coef, b8


def kernel(x_prompt, x_sample, state_conv, state_pool, g_pre, w_in, conv_w, conv_b, conv_ln_g, conv_ln_b,
           w_br_a, gmlp_ln_g, gmlp_ln_b, gmlp_ws, gmlp_bs, w_br_b, pool_w, pool_scale, w_br_c, w_out, g_post):
    depth = w_in.shape[0]
    n_seq, seq_len, _ = x_prompt.shape
    n_dec, dec_seq, _ = x_sample.shape
    assert dec_seq == DEC_SEQ and x_prompt.shape[2] == D_MODEL

    wcat = jnp.concatenate([w_in, w_out], axis=2).astype(bf16)
    w2cat = jnp.concatenate([w_br_a, w_br_b, w_br_c], axis=2).astype(bf16)
    vd = jnp.stack([g_pre, g_post], axis=1)
    ve = jnp.stack([conv_b, conv_ln_g, conv_ln_b, gmlp_ln_g, gmlp_ln_b, pool_scale], axis=1)
    bs_full = jnp.repeat(jnp.transpose(gmlp_bs, (0, 2, 1)), B_GC, axis=-1)
    pw = pool_w.astype(bf16)
    coef, b8 = _sample_mix_tables(gmlp_ws, gmlp_bs)

    xp, conv_p, pool_p = _prompt_group(depth, x_prompt.reshape(n_seq * seq_len, D_MODEL), n_seq, seq_len,
                                       wcat, w2cat, vd, ve, conv_w, gmlp_ws, bs_full, pw)
    xs, conv_s, pool_s, v_s = _sample_group(depth, x_sample.reshape(n_dec * DEC_SEQ, D_MODEL),
                                            wcat, w2cat, vd, ve, conv_w, coef, b8, pw, state_conv, state_pool)
    return (xp.reshape(n_seq, seq_len, D_MODEL), xs.reshape(n_dec, DEC_SEQ, D_MODEL),
            conv_p, pool_p, conv_s, pool_s, v_s)
```

```python
import functools

import jax
import jax.numpy as jnp
import numpy as np
from jax import lax
from jax.experimental import pallas as pl
from jax.experimental.pallas import tpu as pltpu

D_MODEL = 2048
E = 1024
N_BRANCH = 3
CONV_W = 31
CONV_HALO = 32
SUBLANES = 8
LANES = 128
CHUNK = 128
B_GROUPS = 8
B_GC = E // B_GROUPS
POOL_WINDOWS = (2, 4, 8, 16)
C_GC = E // len(POOL_WINDOWS)
POOL_MAX = 16
POOL_HALO = 32
DEC_SEQ = 8
PAST_LEN = 16384
EPS = 1e-6
COL = 1024
N_IN_BLOCKS = (8 * E + N_BRANCH * D_MODEL) // COL
N_BLOCKS = N_IN_BLOCKS + D_MODEL // COL
FIRST_MERGE_BLOCK = 8
N_MERGE = 2 * N_BRANCH
N_SLOTS = 2

TM_PROMPT = 512
TM_SAMPLE = 256
VMEM_LIMIT_BYTES = 60 * 1024 * 1024

f32 = jnp.float32
bf16 = jnp.bfloat16


def _for_rows(n_rows, rc, fn):
    def body(c, carry):
        fn(pl.multiple_of(c * rc, rc))
        return carry
    lax.fori_loop(0, n_rows // rc, body, 0)


def _sigmoid(x):
    return 0.5 * jnp.tanh(0.5 * x) + 0.5


def _layernorm_rows(y, g, b):
    mu = jnp.mean(y, axis=-1, keepdims=True)
    yc = y - mu
    var = jnp.mean(yc * yc, axis=-1, keepdims=True)
    return yc * lax.rsqrt(var + EPS) * g + b


def _start_layer(l, x_ref, vd_ref, out_ref, lhs_ref, tm):
    @pl.when(l == 0)
    def _():
        def load(r0):
            out_ref[pl.ds(r0, 128), :] = x_ref[pl.ds(r0, 128), :]
        _for_rows(tm, 128, load)

    def norm(r0):
        r = pl.ds(r0, 128)
        x = out_ref[r, :]
        ms = jnp.mean(x * x, axis=-1, keepdims=True)
        lhs_ref[r, :] = (x * lax.rsqrt(ms + EPS) * vd_ref[0:1, :]).astype(bf16)
    _for_rows(tm, 128, norm)


def _widen(p_ref, s1_ref, su_ref, tm):
    def go(r0):
        r = pl.ds(r0, 64)
        p = p_ref[r, :]
        su_ref[r, :] = (s1_ref[r, :] * (p * _sigmoid(p))).astype(bf16)
    _for_rows(tm, 64, go)


def _merge(p_ref, su_ref, w2_ref, m_ref, half, first):
    cols = slice(half * COL, (half + 1) * COL)
    y = jnp.dot(su_ref[...], w2_ref[...], preferred_element_type=f32)
    g = _sigmoid(p_ref[...]) * y
    if first:
        m_ref[:, cols] = g
    else:
        m_ref[:, cols] += g


def _m_to_lhs(m_ref, lhs_ref, tm):
    def go(r0):
        r = pl.ds(r0, 64)
        lhs_ref[r, :] = m_ref[r, :].astype(bf16)
    _for_rows(tm, 64, go)


def _post_norm_residual(y0_ref, y1_ref, vd_ref, x_ref, tm):
    def go(r0):
        r = pl.ds(r0, 128)
        y0 = y0_ref[r, :]
        y1 = y1_ref[r, :]
        ss = jnp.sum(y0 * y0, axis=-1, keepdims=True) + jnp.sum(y1 * y1, axis=-1, keepdims=True)
        inv = lax.rsqrt(ss / D_MODEL + EPS)
        x_ref[r, 0:COL] = x_ref[r, 0:COL] + y0 * inv * vd_ref[1:2, 0:COL]
        x_ref[r, COL:D_MODEL] = x_ref[r, COL:D_MODEL] + y1 * inv * vd_ref[1:2, COL:D_MODEL]
    _for_rows(tm, 128, go)


def _ln_rows(src_ref, dst_ref, g, b, tm, swish):
    rc = min(tm, 256)
    def go(r0):
        r = pl.ds(r0, rc)
        y = _layernorm_rows(src_ref[r, :], g, b)
        dst_ref[r, :] = y * _sigmoid(y) if swish else y
    _for_rows(tm, rc, go)


def _run_tile_layer(w_hbm, w2_hbm, x_ref, vd_ref, out_ref,
                    lhs_ref, p_ref, s1_ref, su_ref, m_ref, wbuf_ref, w2buf_ref, wsem, w2sem,
                    tm, branch_steps):
    i, l = pl.program_id(0), pl.program_id(1)
    n_tiles, depth = pl.num_programs(0), pl.num_programs(1)
    is_first = jnp.logical_and(i == 0, l == 0)
    is_last = jnp.logical_and(i == n_tiles - 1, l == depth - 1)
    next_l = jnp.where(l + 1 < depth, l + 1, 0)

    def w_copy(layer, k):
        slot = k % N_SLOTS
        return pltpu.make_async_copy(w_hbm.at[layer, :, pl.ds(k * COL, COL)], wbuf_ref.at[slot], wsem.at[slot])

    def w2_copy(j):
        slot = j % N_SLOTS
        return pltpu.make_async_copy(w2_hbm.at[l, :, pl.ds(j * COL, COL)], w2buf_ref.at[slot], w2sem.at[slot])

    @pl.when(is_first)
    def _():
        w_copy(l, 0).start()

    _start_layer(l, x_ref, vd_ref, out_ref, lhs_ref, tm)

    for k in range(N_BLOCKS):
        w_copy(l, k).wait()
        if k + 1 < N_BLOCKS:
            w_copy(l, k + 1).start()
        else:
            @pl.when(jnp.logical_not(is_last))
            def _():
                w_copy(next_l, 0).start()
        if k == FIRST_MERGE_BLOCK - 1:
            w2_copy(0).start()

        p_ref[k % 2] = jnp.dot(lhs_ref[...], wbuf_ref[k % N_SLOTS], preferred_element_type=f32)

        if k in branch_steps:
            branch_steps[k]()
        if k in (2, 5, 7):
            _widen(p_ref.at[k % 2], s1_ref, su_ref.at[(2, 5, 7).index(k)], tm)
        if FIRST_MERGE_BLOCK <= k < N_IN_BLOCKS:
            j = k - FIRST_MERGE_BLOCK
            w2_copy(j).wait()
            if j + 1 < N_MERGE:
                w2_copy(j + 1).start()
            _merge(p_ref.at[k % 2], su_ref.at[j // 2], w2buf_ref.at[j % N_SLOTS], m_ref, j % 2, first=j < 2)
        if k == N_IN_BLOCKS - 1:
            _m_to_lhs(m_ref, lhs_ref, tm)
        if k == N_BLOCKS - 1:
            _post_norm_residual(p_ref.at[0], p_ref.at[1], vd_ref, out_ref, tm)


def _conv_rows_prompt(zbuf_ref, cw_ref, ve_ref, dst_ref, r0, rc):
    for cb in range(E // LANES):
        cols = slice(cb * LANES, (cb + 1) * LANES)
        blk = zbuf_ref[pl.ds(r0, rc + CONV_HALO), cols]
        lead = CONV_HALO - (CONV_W - 1)
        y = ve_ref[0:1, cols] + cw_ref[CONV_W - 1:CONV_W, cols] * blk[CONV_HALO:CONV_HALO + rc, :]
        for b in range(SUBLANES):
            acc = None
            for a in range(CONV_HALO // SUBLANES):
                k = SUBLANES * a + b - lead
                if k < 0:
                    continue
                term = cw_ref[k:k + 1, cols] * blk[SUBLANES * a:SUBLANES * a + rc + SUBLANES, :]
                acc = term if acc is None else acc + term
            y = y + acc[b:b + rc, :]
        dst_ref[pl.ds(r0, rc), cols] = y


def _trailing_sum(blk, w):
    rows = blk.shape[0]
    win, lo, span = blk, 0, 1
    while span < w:
        n = rows - (lo + SUBLANES)
        win = win[SUBLANES:SUBLANES + n, :] + win[SUBLANES - span:SUBLANES - span + n, :]
        lo += SUBLANES
        span *= 2
    return win[POOL_HALO - lo:, :]


def _prompt_kernel(x_ref, w_hbm, w2_hbm, vd_ref, ve_ref, cw_ref, ws_ref, bs_ref, pw_ref,
                   out_ref, cst_ref, pst_ref,
                   lhs_ref, p_ref, s1_ref, su_ref, m_ref, wbuf_ref, w2buf_ref, wsem, w2sem,
                   zbuf_ref, cbuf_ref, zhalo_ref, chalo_ref,
                   *, tiles_per_seq):
    tm = TM_PROMPT
    l = pl.program_id(1)
    q = pl.program_id(0) % tiles_per_seq

    def conv_branch():
        @pl.when(q == 0)
        def _():
            zbuf_ref[0:CONV_HALO, :] = jnp.zeros((CONV_HALO, E), f32)

        @pl.when(q > 0)
        def _():
            zbuf_ref[0:CONV_HALO, :] = zhalo_ref[l]

        def glu(r0):
            r = pl.ds(r0, 64)
            zbuf_ref[pl.ds(CONV_HALO + r0, 64), :] = p_ref[0, r, :] * _sigmoid(p_ref[1, r, :])
        _for_rows(tm, 64, glu)

        _for_rows(tm, CHUNK, lambda r0: _conv_rows_prompt(zbuf_ref, cw_ref, ve_ref, s1_ref, r0, CHUNK))
        _ln_rows(s1_ref, s1_ref, ve_ref[1:2, :], ve_ref[2:3, :], tm, swish=True)

        cst_ref[...] = zbuf_ref[CONV_HALO + tm - (CONV_W - 1):CONV_HALO + tm, :]
        zhalo_ref[l] = zbuf_ref[tm:tm + CONV_HALO, :]

    def gmlp_branch():
        _ln_rows(p_ref.at[0], s1_ref, ve_ref[3:4, :], ve_ref[4:5, :], tm, swish=False)
        tril = (lax.broadcasted_iota(jnp.int32, (CHUNK, CHUNK), 0)
                >= lax.broadcasted_iota(jnp.int32, (CHUNK, CHUNK), 1))
        for grp in range(B_GROUPS):
            cols = slice(grp * B_GC, (grp + 1) * B_GC)
            w_g = jnp.where(tril, ws_ref[grp], 0.0).astype(bf16)
            for c in range(tm // CHUNK):
                rows = slice(c * CHUNK, (c + 1) * CHUNK)
                v = s1_ref[rows, cols].astype(bf16)
                mixed = jnp.dot(w_g, v, preferred_element_type=f32) + bs_ref[:, cols]
                s1_ref[rows, cols] = p_ref[1, rows, cols] * mixed

    def pool_branch():
        @pl.when(q == 0)
        def _():
            cbuf_ref[0:POOL_HALO, :] = jnp.zeros((POOL_HALO, E), f32)

        @pl.when(q > 0)
        def _():
            cbuf_ref[0:POOL_HALO, :] = chalo_ref[l]

        def stash(r0):
            cbuf_ref[pl.ds(POOL_HALO + r0, 64), :] = p_ref[0, pl.ds(r0, 64), :]
        _for_rows(tm, 64, stash)

        def pool(r0):
            pos = q * tm + r0 + lax.broadcasted_iota(jnp.int32, (CHUNK, C_GC), 0)
            for gi, w in enumerate(POOL_WINDOWS):
                cols = slice(gi * C_GC, (gi + 1) * C_GC)
                blk = cbuf_ref[pl.ds(r0, POOL_HALO + CHUNK), cols]
                c = blk[POOL_HALO:, :]
                cnt = jnp.minimum(w, pos + 1).astype(f32)
                d = _trailing_sum(blk, w) / cnt - c
                y = jnp.dot(d.astype(bf16), pw_ref[gi], preferred_element_type=f32)
                s1_ref[pl.ds(r0, CHUNK), cols] = y * ve_ref[5:6, cols]
        _for_rows(tm, CHUNK, pool)

        pst_ref[...] = cbuf_ref[POOL_HALO + tm - (POOL_MAX - 1):POOL_HALO + tm, :]
        chalo_ref[l] = cbuf_ref[tm:tm + POOL_HALO, :]

    _run_tile_layer(w_hbm, w2_hbm, x_ref, vd_ref, out_ref,
                    lhs_ref, p_ref, s1_ref, su_ref, m_ref, wbuf_ref, w2buf_ref, wsem, w2sem,
                    tm, {1: conv_branch, 4: gmlp_branch, 6: pool_branch})


def _sample_kernel(x_ref, w_hbm, w2_hbm, vd_ref, ve_ref, cw_ref, coef_ref, b8_ref, pw_ref,
                   stc_ref, stp_ref,
                   out_ref, cso_ref, pso_ref, v_ref,
                   lhs_ref, p_ref, s1_ref, su_ref, m_ref, wbuf_ref, w2buf_ref, wsem, w2sem,
                   tail_ref):
    tm = TM_SAMPLE
    n_seq = tm // DEC_SEQ
    sb = 8
    conv_keep = CONV_W - 1 - DEC_SEQ
    pool_keep = POOL_MAX - 1 - DEC_SEQ

    def rows_of(q0):
        return pl.ds(pl.multiple_of(q0 * DEC_SEQ, sb * DEC_SEQ), sb * DEC_SEQ)

    def conv_branch():
        def go(q0):
            qs = pl.ds(q0, sb)
            r = rows_of(q0)
            z = (p_ref[0, r, :] * _sigmoid(p_ref[1, r, :])).reshape(sb, DEC_SEQ, E)
            cso_ref[qs, 0:conv_keep, :] = stc_ref[qs, DEC_SEQ:CONV_W - 1, :]
            cso_ref[qs, conv_keep:CONV_W - 1, :] = z
            tail_ref[qs, 0:8, :] = stc_ref[qs, conv_keep:CONV_W - 1, :]
            tail_ref[qs, 8:16, :] = z
            for cb in range(E // 256):
                cols = slice(cb * 256, (cb + 1) * 256)
                acc = jnp.broadcast_to(ve_ref[0:1, cols].reshape(1, 1, 256), (sb, DEC_SEQ, 256))
                for k in range(CONV_W):
                    wk = cw_ref[k:k + 1, cols].reshape(1, 1, 256)
                    if k + DEC_SEQ <= CONV_W - 1:
                        win = stc_ref[qs, k:k + DEC_SEQ, cols]
                    else:
                        o = k - conv_keep
                        win = tail_ref[qs, o:o + DEC_SEQ, cols]
                    acc = acc + wk * win
                s1_ref[r, cols] = acc.reshape(sb * DEC_SEQ, 256)
        _for_rows(n_seq, sb, go)
        _ln_rows(s1_ref, s1_ref, ve_ref[1:2, :], ve_ref[2:3, :], tm, swish=True)

    def gmlp_branch():
        _ln_rows(p_ref.at[0], s1_ref, ve_ref[3:4, :], ve_ref[4:5, :], tm, swish=False)

        def go(q0):
            qs = pl.ds(q0, sb)
            r = rows_of(q0)
            v = s1_ref[r, :].reshape(sb, DEC_SEQ, E)
            v_ref[qs, :, :] = v
            tail_ref[qs, 0:8, :] = jnp.zeros((sb, 8, E), f32)
            tail_ref[qs, 8:16, :] = v
            for cb in range(E // 256):
                cols = slice(cb * 256, (cb + 1) * 256)
                acc = jnp.broadcast_to(b8_ref[:, cols].reshape(1, DEC_SEQ, 256), (sb, DEC_SEQ, 256))
                for j in range(DEC_SEQ):
                    acc = acc + coef_ref[j, :, cols].reshape(1, DEC_SEQ, 256) * tail_ref[qs, 8 - j:16 - j, cols]
                s1_ref[r, cols] = p_ref[1, r, cols] * acc.reshape(sb * DEC_SEQ, 256)
        _for_rows(n_seq, sb, go)

    def pool_branch():
        def go(q0):
            qs = pl.ds(q0, sb)
            r = rows_of(q0)
            c3 = p_ref[0, r, :].reshape(sb, DEC_SEQ, E)
            pso_ref[qs, 0:pool_keep, :] = stp_ref[qs, DEC_SEQ:POOL_MAX - 1, :]
            pso_ref[qs, pool_keep:POOL_MAX - 1, :] = c3
            tail_ref[qs, 0:8, :] = stp_ref[qs, pool_keep:POOL_MAX - 1, :]
            tail_ref[qs, 8:16, :] = c3
            for gi, w in enumerate(POOL_WINDOWS):
                cols = slice(gi * C_GC, (gi + 1) * C_GC)
                c = tail_ref[qs, 8:16, cols]
                win = c
                for j in range(1, w):
                    if j < 8:
                        win = win + tail_ref[qs, 8 - j:16 - j, cols]
                    else:
                        win = win + stp_ref[qs, POOL_MAX - 1 - j:POOL_MAX - 1 - j + DEC_SEQ, cols]
                d = win / float(min(w, PAST_LEN + 1)) - c
                s1_ref[r, cols] = d.reshape(sb * DEC_SEQ, C_GC)
        _for_rows(n_seq, sb, go)

        for gi in range(len(POOL_WINDOWS)):
            cols = slice(gi * C_GC, (gi + 1) * C_GC)
            y = jnp.dot(s1_ref[:, cols].astype(bf16), pw_ref[gi], preferred_element_type=f32)
            s1_ref[:, cols] = y * ve_ref[5:6, cols]

    _run_tile_layer(w_hbm, w2_hbm, x_ref, vd_ref, out_ref,
                    lhs_ref, p_ref, s1_ref, su_ref, m_ref, wbuf_ref, w2buf_ref, wsem, w2sem,
                    tm, {1: conv_branch, 4: gmlp_branch, 6: pool_branch})


def _weight_specs():
    return [
        pl.BlockSpec(memory_space=pl.ANY),
        pl.BlockSpec(memory_space=pl.ANY),
        pl.BlockSpec((None, 2, D_MODEL), lambda i, l: (l, 0, 0)),
        pl.BlockSpec((None, 6, E), lambda i, l: (l, 0, 0)),
        pl.BlockSpec((None, CONV_W, E), lambda i, l: (l, 0, 0)),
    ]


def _work_buffers(tm):
    return [
        pltpu.VMEM((tm, D_MODEL), bf16),
        pltpu.VMEM((2, tm, COL), f32),
        pltpu.VMEM((tm, E), f32),
        pltpu.VMEM((N_BRANCH, tm, E), bf16),
        pltpu.VMEM((tm, D_MODEL), f32),
        pltpu.VMEM((N_SLOTS, D_MODEL, COL), bf16),
        pltpu.VMEM((N_SLOTS, E, COL), bf16),
        pltpu.SemaphoreType.DMA((N_SLOTS,)),
        pltpu.SemaphoreType.DMA((N_SLOTS,)),
    ]


def _compiler_params():
    return pltpu.CompilerParams(dimension_semantics=("arbitrary", "arbitrary"),
                                vmem_limit_bytes=VMEM_LIMIT_BYTES)


def _prompt_group(depth, x2d, n_seq, seq_len, wcat, w2cat, vd, ve, cw, ws, bs_full, pw):
    tm = TM_PROMPT
    assert seq_len % tm == 0 and tm % CHUNK == 0
    tiles_per_seq = seq_len // tm
    n_rows = n_seq * seq_len
    grid = (n_rows // tm, depth)
    in_specs = [pl.BlockSpec((tm, D_MODEL), lambda i, l: (i, 0))] + _weight_specs() + [
        pl.BlockSpec((None, B_GROUPS, CHUNK, CHUNK), lambda i, l: (l, 0, 0, 0)),
        pl.BlockSpec((None, CHUNK, E), lambda i, l: (l, 0, 0)),
        pl.BlockSpec((None, len(POOL_WINDOWS), C_GC, C_GC), lambda i, l: (l, 0, 0, 0)),
    ]
    out_specs = [
        pl.BlockSpec((tm, D_MODEL), lambda i, l: (i, 0)),
        pl.BlockSpec((None, None, CONV_W - 1, E), lambda i, l: (l, i, 0, 0)),
        pl.BlockSpec((None, None, POOL_MAX - 1, E), lambda i, l: (l, i, 0, 0)),
    ]
    out_shape = [
        jax.ShapeDtypeStruct((n_rows, D_MODEL), f32),
        jax.ShapeDtypeStruct((depth, n_rows // tm, CONV_W - 1, E), f32),
        jax.ShapeDtypeStruct((depth, n_rows // tm, POOL_MAX - 1, E), f32),
    ]
    scratch = _work_buffers(tm) + [
        pltpu.VMEM((CONV_HALO + tm, E), f32),
        pltpu.VMEM((POOL_HALO + tm, E), f32),
        pltpu.VMEM((depth, CONV_HALO, E), f32),
        pltpu.VMEM((depth, POOL_HALO, E), f32),
    ]
    x_out, conv_tiles, pool_tiles = pl.pallas_call(
        functools.partial(_prompt_kernel, tiles_per_seq=tiles_per_seq),
        grid=grid, in_specs=in_specs, out_specs=out_specs, out_shape=out_shape,
        scratch_shapes=scratch, compiler_params=_compiler_params(),
        name="prompt_group",
    )(x2d, wcat, w2cat, vd, ve, cw, ws, bs_full, pw)
    last = slice(tiles_per_seq - 1, None, tiles_per_seq)
    return x_out, conv_tiles[:, last], pool_tiles[:, last]


def _sample_group(depth, x2d, wcat, w2cat, vd, ve, cw, coef, b8, pw, state_conv, state_pool):
    tm = TM_SAMPLE
    n_rows = x2d.shape[0]
    n_seq_total = n_rows // DEC_SEQ
    sq = tm // DEC_SEQ
    assert n_rows % tm == 0
    grid = (n_rows // tm, depth)
    in_specs = [pl.BlockSpec((tm, D_MODEL), lambda i, l: (i, 0))] + _weight_specs() + [
        pl.BlockSpec((None, DEC_SEQ, DEC_SEQ, E), lambda i, l: (l, 0, 0, 0)),
        pl.BlockSpec((None, DEC_SEQ, E), lambda i, l: (l, 0, 0)),
        pl.BlockSpec((None, len(POOL_WINDOWS), C_GC, C_GC), lambda i, l: (l, 0, 0, 0)),
        pl.BlockSpec((None, sq, CONV_W - 1, E), lambda i, l: (l, i, 0, 0)),
        pl.BlockSpec((None, sq, POOL_MAX - 1, E), lambda i, l: (l, i, 0, 0)),
    ]
    out_specs = [
        pl.BlockSpec((tm, D_MODEL), lambda i, l: (i, 0)),
        pl.BlockSpec((None, sq, CONV_W - 1, E), lambda i, l: (l, i, 0, 0)),
        pl.BlockSpec((None, sq, POOL_MAX - 1, E), lambda i, l: (l, i, 0, 0)),
        pl.BlockSpec((None, sq, DEC_SEQ, E), lambda i, l: (l, i, 0, 0)),
    ]
    out_shape = [
        jax.ShapeDtypeStruct((n_rows, D_MODEL), f32),
        jax.ShapeDtypeStruct((depth, n_seq_total, CONV_W - 1, E), f32),
        jax.ShapeDtypeStruct((depth, n_seq_total, POOL_MAX - 1, E), f32),
        jax.ShapeDtypeStruct((depth, n_seq_total, DEC_SEQ, E), f32),
    ]
    scratch = _work_buffers(tm) + [
        pltpu.VMEM((sq, 16, E), f32),
    ]
    return pl.pallas_call(
        _sample_kernel,
        grid=grid, in_specs=in_specs, out_specs=out_specs, out_shape=out_shape,
        scratch_shapes=scratch, compiler_params=_compiler_params(),
        name="sample_group",
    )(x2d, wcat, w2cat, vd, ve, cw, coef, b8, pw, state_conv, state_pool)


def _sample_mix_tables(gmlp_ws, gmlp_bs):
    t = np.arange(DEC_SEQ)
    pick = (t[None, None, :] == t[None, :, None] - t[:, None, None]).astype(np.float32)
    w = jnp.sum(gmlp_ws[:, :, None, :DEC_SEQ, :DEC_SEQ] * pick[None, None], axis=-1)
    coef = jnp.repeat(jnp.transpose(w, (0, 2, 3, 1)), B_GC, axis=-1)
    b8 = jnp.repeat(jnp.transpose(gmlp_bs[:, :, :DEC_SEQ], (0, 2, 1)), B_GC, axis=-1)
    return coef, b8


def kernel(x_prompt, x_sample, state_conv, state_pool, g_pre, w_in, conv_w, conv_b, conv_ln_g, conv_ln_b,
           w_br_a, gmlp_ln_g, gmlp_ln_b, gmlp_ws, gmlp_bs, w_br_b, pool_w, pool_scale, w_br_c, w_out, g_post):
    depth = w_in.shape[0]
    n_seq, seq_len, _ = x_prompt.shape
    n_dec, dec_seq, _ = x_sample.shape
    assert dec_seq == DEC_SEQ and x_prompt.shape[2] == D_MODEL

    wcat = jnp.concatenate([w_in, w_out], axis=2).astype(bf16)
    w2cat = jnp.concatenate([w_br_a, w_br_b, w_br_c], axis=2).astype(bf16)
    vd = jnp.stack([g_pre, g_post], axis=1)
    ve = jnp.stack([conv_b, conv_ln_g, conv_ln_b, gmlp_ln_g, gmlp_ln_b, pool_scale], axis=1)
    bs_full = jnp.repeat(jnp.transpose(gmlp_bs, (0, 2, 1)), B_GC, axis=-1)
    pw = pool_w.astype(bf16)
    coef, b8 = _sample_mix_tables(gmlp_ws, gmlp_bs)

    xp, conv_p, pool_p = _prompt_group(depth, x_prompt.reshape(n_seq * seq_len, D_MODEL), n_seq, seq_len,
                                       wcat, w2cat, vd, ve, conv_w, gmlp_ws, bs_full, pw)
    xs, conv_s, pool_s, v_s = _sample_group(depth, x_sample.reshape(n_dec * DEC_SEQ, D_MODEL),
                                            wcat, w2cat, vd, ve, conv_w, coef, b8, pw, state_conv, state_pool)
    return (xp.reshape(n_seq, seq_len, D_MODEL), xs.reshape(n_dec, DEC_SEQ, D_MODEL),
            conv_p, pool_p, conv_s, pool_s, v_s)
```

```python
import functools

import jax
import jax.numpy as jnp
import numpy as np
from jax import lax
from jax.experimental import pallas as pl
from jax.experimental.pallas import tpu as pltpu

D_MODEL = 2048
E = 1024
N_BRANCH = 3
CONV_W = 31
CONV_HALO = 32
SUBLANES = 8
LANES = 128
CHUNK = 128
B_GROUPS = 8
B_GC = E // B_GROUPS
POOL_WINDOWS = (2, 4, 8, 16)
C_GC = E // len(POOL_WINDOWS)
POOL_MAX = 16
POOL_HALO = 32
DEC_SEQ = 8
PAST_LEN = 16384
EPS = 1e-6
COL = 1024
N_IN_BLOCKS = (8 * E + N_BRANCH * D_MODEL) // COL
N_BLOCKS = N_IN_BLOCKS + D_MODEL // COL
FIRST_MERGE_BLOCK = 8
N_MERGE = 2 * N_BRANCH
N_SLOTS = 2

TM_PROMPT = 512
TM_SAMPLE = 256
VMEM_LIMIT_BYTES = 60 * 1024 * 1024

f32 = jnp.float32
bf16 = jnp.bfloat16


def _for_rows(n_rows, rc, fn):
    def body(c, carry):
        fn(pl.multiple_of(c * rc, rc))
        return carry
    lax.fori_loop(0, n_rows // rc, body, 0)


def _sigmoid(x):
    return 0.5 * jnp.tanh(0.5 * x) + 0.5


def _layernorm_rows(y, g, b):
    mu = jnp.mean(y, axis=-1, keepdims=True)
    yc = y - mu
    var = jnp.mean(yc * yc, axis=-1, keepdims=True)
    return yc * lax.rsqrt(var + EPS) * g + b


def _start_layer(l, x_ref, vd_ref, out_ref, lhs_ref, tm):
    @pl.when(l == 0)
    def _():
        def load(r0):
            out_ref[pl.ds(r0, 128), :] = x_ref[pl.ds(r0, 128), :]
        _for_rows(tm, 128, load)

    def norm(r0):
        r = pl.ds(r0, 128)
        x = out_ref[r, :]
        ms = jnp.mean(x * x, axis=-1, keepdims=True)
        lhs_ref[r, :] = (x * lax.rsqrt(ms + EPS) * vd_ref[0:1, :]).astype(bf16)
    _for_rows(tm, 128, norm)


def _widen(p_ref, s1_ref, su_ref, tm):
    def go(r0):
        r = pl.ds(r0, 64)
        p = p_ref[r, :]
        su_ref[r, :] = (s1_ref[r, :] * (p * _sigmoid(p))).astype(bf16)
    _for_rows(tm, 64, go)


def _merge(p_ref, su_ref, w2_ref, m_ref, half, first):
    cols = slice(half * COL, (half + 1) * COL)
    y = jnp.dot(su_ref[...], w2_ref[...], preferred_element_type=f32)
    g = _sigmoid(p_ref[...]) * y
    if first:
        m_ref[:, cols] = g
    else:
        m_ref[:, cols] += g


def _m_to_lhs(m_ref, lhs_ref, tm):
    def go(r0):
        r = pl.ds(r0, 64)
        lhs_ref[r, :] = m_ref[r, :].astype(bf16)
    _for_rows(tm, 64, go)


def _post_norm_residual(y0_ref, y1_ref, vd_ref, x_ref, tm):
    def go(r0):
        r = pl.ds(r0, 128)
        y0 = y0_ref[r, :]
        y1 = y1_ref[r, :]
        ss = jnp.sum(y0 * y0, axis=-1, keepdims=True) + jnp.sum(y1 * y1, axis=-1, keepdims=True)
        inv = lax.rsqrt(ss / D_MODEL + EPS)
        x_ref[r, 0:COL] = x_ref[r, 0:COL] + y0 * inv * vd_ref[1:2, 0:COL]
        x_ref[r, COL:D_MODEL] = x_ref[r, COL:D_MODEL] + y1 * inv * vd_ref[1:2, COL:D_MODEL]
    _for_rows(tm, 128, go)


def _ln_rows(src_ref, dst_ref, g, b, tm, swish):
    rc = min(tm, 256)
    def go(r0):
        r = pl.ds(r0, rc)
        y = _layernorm_rows(src_ref[r, :], g, b)
        dst_ref[r, :] = y * _sigmoid(y) if swish else y
    _for_rows(tm, rc, go)


def _run_tile_layer(weights, x_ref, vd_ref, out_ref,
                    lhs_ref, p_ref, s1_ref, su_ref, m_ref, wbuf_ref, w2buf_ref, wsem, w2sem,
                    tm, branch_steps):
    w_in_hbm, w_out_hbm = weights[0], weights[1]
    w_br_hbm = weights[2:]
    i, l = pl.program_id(0), pl.program_id(1)
    n_tiles, depth = pl.num_programs(0), pl.num_programs(1)
    is_first = jnp.logical_and(i == 0, l == 0)
    is_last = jnp.logical_and(i == n_tiles - 1, l == depth - 1)
    next_l = jnp.where(l + 1 < depth, l + 1, 0)

    def w_copy(layer, k):
        slot = k % N_SLOTS
        src, kk = (w_in_hbm, k) if k < N_IN_BLOCKS else (w_out_hbm, k - N_IN_BLOCKS)
        return pltpu.make_async_copy(src.at[layer, :, pl.ds(kk * COL, COL)], wbuf_ref.at[slot], wsem.at[slot])

    def w2_copy(j):
        slot = j % N_SLOTS
        return pltpu.make_async_copy(w_br_hbm[j // 2].at[l, :, pl.ds((j % 2) * COL, COL)],
                                     w2buf_ref.at[slot], w2sem.at[slot])

    @pl.when(is_first)
    def _():
        w_copy(l, 0).start()

    _start_layer(l, x_ref, vd_ref, out_ref, lhs_ref, tm)

    for k in range(N_BLOCKS):
        w_copy(l, k).wait()
        if k + 1 < N_BLOCKS:
            w_copy(l, k + 1).start()
        else:
            @pl.when(jnp.logical_not(is_last))
            def _():
                w_copy(next_l, 0).start()
        if k == FIRST_MERGE_BLOCK - 1:
            w2_copy(0).start()

        p_ref[k % 2] = jnp.dot(lhs_ref[...], wbuf_ref[k % N_SLOTS], preferred_element_type=f32)

        if k in branch_steps:
            branch_steps[k]()
        if k in (2, 5, 7):
            _widen(p_ref.at[k % 2], s1_ref, su_ref.at[(2, 5, 7).index(k)], tm)
        if FIRST_MERGE_BLOCK <= k < N_IN_BLOCKS:
            j = k - FIRST_MERGE_BLOCK
            w2_copy(j).wait()
            if j + 1 < N_MERGE:
                w2_copy(j + 1).start()
            _merge(p_ref.at[k % 2], su_ref.at[j // 2], w2buf_ref.at[j % N_SLOTS], m_ref, j % 2, first=j < 2)
        if k == N_IN_BLOCKS - 1:
            _m_to_lhs(m_ref, lhs_ref, tm)
        if k == N_BLOCKS - 1:
            _post_norm_residual(p_ref.at[0], p_ref.at[1], vd_ref, out_ref, tm)


def _conv_rows_prompt(zbuf_ref, cw_ref, ve_ref, dst_ref, r0, rc):
    for cb in range(E // LANES):
        cols = slice(cb * LANES, (cb + 1) * LANES)
        blk = zbuf_ref[pl.ds(r0, rc + CONV_HALO), cols]
        lead = CONV_HALO - (CONV_W - 1)
        y = ve_ref[0:1, cols] + cw_ref[CONV_W - 1:CONV_W, cols] * blk[CONV_HALO:CONV_HALO + rc, :]
        for b in range(SUBLANES):
            acc = None
            for a in range(CONV_HALO // SUBLANES):
                k = SUBLANES * a + b - lead
                if k < 0:
                    continue
                term = cw_ref[k:k + 1, cols] * blk[SUBLANES * a:SUBLANES * a + rc + SUBLANES, :]
                acc = term if acc is None else acc + term
            y = y + acc[b:b + rc, :]
        dst_ref[pl.ds(r0, rc), cols] = y


def _trailing_sum(blk, w):
    rows = blk.shape[0]
    win, lo, span = blk, 0, 1
    while span < w:
        n = rows - (lo + SUBLANES)
        win = win[SUBLANES:SUBLANES + n, :] + win[SUBLANES - span:SUBLANES - span + n, :]
        lo += SUBLANES
        span *= 2
    return win[POOL_HALO - lo:, :]


def _prompt_kernel(x_ref, w_in_hbm, w_out_hbm, w_br_a_hbm, w_br_b_hbm, w_br_c_hbm,
                   vd_ref, ve_ref, cw_ref, ws_ref, bs_ref, pw_ref,
                   out_ref, cst_ref, pst_ref,
                   lhs_ref, p_ref, s1_ref, su_ref, m_ref, wbuf_ref, w2buf_ref, wsem, w2sem,
                   zbuf_ref, cbuf_ref, zhalo_ref, chalo_ref,
                   *, tiles_per_seq):
    tm = TM_PROMPT
    l = pl.program_id(1)
    q = pl.program_id(0) % tiles_per_seq

    def conv_branch():
        @pl.when(q == 0)
        def _():
            zbuf_ref[0:CONV_HALO, :] = jnp.zeros((CONV_HALO, E), f32)

        @pl.when(q > 0)
        def _():
            zbuf_ref[0:CONV_HALO, :] = zhalo_ref[l]

        def glu(r0):
            r = pl.ds(r0, 64)
            zbuf_ref[pl.ds(CONV_HALO + r0, 64), :] = p_ref[0, r, :] * _sigmoid(p_ref[1, r, :])
        _for_rows(tm, 64, glu)

        _for_rows(tm, CHUNK, lambda r0: _conv_rows_prompt(zbuf_ref, cw_ref, ve_ref, s1_ref, r0, CHUNK))
        _ln_rows(s1_ref, s1_ref, ve_ref[1:2, :], ve_ref[2:3, :], tm, swish=True)

        cst_ref[...] = zbuf_ref[CONV_HALO + tm - (CONV_W - 1):CONV_HALO + tm, :]
        zhalo_ref[l] = zbuf_ref[tm:tm + CONV_HALO, :]

    def gmlp_branch():
        _ln_rows(p_ref.at[0], s1_ref, ve_ref[3:4, :], ve_ref[4:5, :], tm, swish=False)
        tril = (lax.broadcasted_iota(jnp.int32, (CHUNK, CHUNK), 0)
                >= lax.broadcasted_iota(jnp.int32, (CHUNK, CHUNK), 1))
        for grp in range(B_GROUPS):
            cols = slice(grp * B_GC, (grp + 1) * B_GC)
            w_g = jnp.where(tril, ws_ref[grp], 0.0).astype(bf16)
            for c in range(tm // CHUNK):
                rows = slice(c * CHUNK, (c + 1) * CHUNK)
                v = s1_ref[rows, cols].astype(bf16)
                mixed = jnp.dot(w_g, v, preferred_element_type=f32) + bs_ref[:, cols]
                s1_ref[rows, cols] = p_ref[1, rows, cols] * mixed

    def pool_branch():
        @pl.when(q == 0)
        def _():
            cbuf_ref[0:POOL_HALO, :] = jnp.zeros((POOL_HALO, E), f32)

        @pl.when(q > 0)
        def _():
            cbuf_ref[0:POOL_HALO, :] = chalo_ref[l]

        def stash(r0):
            cbuf_ref[pl.ds(POOL_HALO + r0, 64), :] = p_ref[0, pl.ds(r0, 64), :]
        _for_rows(tm, 64, stash)

        def pool(r0):
            pos = q * tm + r0 + lax.broadcasted_iota(jnp.int32, (CHUNK, C_GC), 0)
            for gi, w in enumerate(POOL_WINDOWS):
                cols = slice(gi * C_GC, (gi + 1) * C_GC)
                blk = cbuf_ref[pl.ds(r0, POOL_HALO + CHUNK), cols]
                c = blk[POOL_HALO:, :]
                cnt = jnp.minimum(w, pos + 1).astype(f32)
                d = _trailing_sum(blk, w) / cnt - c
                y = jnp.dot(d.astype(bf16), pw_ref[gi], preferred_element_type=f32)
                s1_ref[pl.ds(r0, CHUNK), cols] = y * ve_ref[5:6, cols]
        _for_rows(tm, CHUNK, pool)

        pst_ref[...] = cbuf_ref[POOL_HALO + tm - (POOL_MAX - 1):POOL_HALO + tm, :]
        chalo_ref[l] = cbuf_ref[tm:tm + POOL_HALO, :]

    _run_tile_layer((w_in_hbm, w_out_hbm, w_br_a_hbm, w_br_b_hbm, w_br_c_hbm), x_ref, vd_ref, out_ref,
                    lhs_ref, p_ref, s1_ref, su_ref, m_ref, wbuf_ref, w2buf_ref, wsem, w2sem,
                    tm, {1: conv_branch, 4: gmlp_branch, 6: pool_branch})


def _sample_kernel(x_ref, w_in_hbm, w_out_hbm, w_br_a_hbm, w_br_b_hbm, w_br_c_hbm,
                   vd_ref, ve_ref, cw_ref, mix_ref, b8_ref, pw_ref, stc_ref, stp_ref,
                   out_ref, cso_ref, pso_ref, v_ref,
                   lhs_ref, p_ref, s1_ref, su_ref, m_ref, wbuf_ref, w2buf_ref, wsem, w2sem):
    tm = TM_SAMPLE
    sq = tm // DEC_SEQ
    cbw = 256

    def slab(t):
        return slice(t * sq, (t + 1) * sq)

    def conv_branch():
        def glu(r0):
            r = pl.ds(r0, 64)
            m_ref[r, 0:E] = p_ref[0, r, :] * _sigmoid(p_ref[1, r, :])
        _for_rows(tm, 64, glu)

        def zp(j, cols):
            if j < CONV_W - 1:
                return stc_ref[j, :, cols]
            return m_ref[slab(j - (CONV_W - 1)), cols]

        for j in range(CONV_W - 1):
            cso_ref[j] = zp(j + DEC_SEQ, slice(0, E))
        for cb in range(E // cbw):
            cols = slice(cb * cbw, (cb + 1) * cbw)
            for t in range(DEC_SEQ):
                acc = jnp.broadcast_to(ve_ref[0:1, cols], (sq, cbw))
                for k in range(CONV_W):
                    acc = acc + cw_ref[k:k + 1, cols] * zp(t + k, cols)
                s1_ref[slab(t), cols] = acc
        _ln_rows(s1_ref, s1_ref, ve_ref[1:2, :], ve_ref[2:3, :], tm, swish=True)

    def gmlp_branch():
        _ln_rows(p_ref.at[0], m_ref.at[:, 0:E], ve_ref[3:4, :], ve_ref[4:5, :], tm, swish=False)
        for t in range(DEC_SEQ):
            v_ref[t] = m_ref[slab(t), 0:E]
        for cb in range(E // cbw):
            cols = slice(cb * cbw, (cb + 1) * cbw)
            for t in range(DEC_SEQ):
                acc = jnp.broadcast_to(b8_ref[t:t + 1, cols], (sq, cbw))
                for s in range(t + 1):
                    acc = acc + mix_ref[t, s:s + 1, cols] * m_ref[slab(s), cols]
                s1_ref[slab(t), cols] = p_ref[1, slab(t), cols] * acc

    def pool_branch():
        def zc(j, cols):
            if j < POOL_MAX - 1:
                return stp_ref[j, :, cols]
            return p_ref[0, slab(j - (POOL_MAX - 1)), cols]

        for j in range(POOL_MAX - 1):
            pso_ref[j] = zc(j + DEC_SEQ, slice(0, E))
        for gi, w in enumerate(POOL_WINDOWS):
            cols = slice(gi * C_GC, (gi + 1) * C_GC)
            for t in range(DEC_SEQ):
                c = zc(POOL_MAX - 1 + t, cols)
                win = c
                for j in range(1, w):
                    win = win + zc(POOL_MAX - 1 + t - j, cols)
                s1_ref[slab(t), cols] = win / float(min(w, PAST_LEN + 1)) - c
        for gi in range(len(POOL_WINDOWS)):
            cols = slice(gi * C_GC, (gi + 1) * C_GC)
            y = jnp.dot(s1_ref[:, cols].astype(bf16), pw_ref[gi], preferred_element_type=f32)
            s1_ref[:, cols] = y * ve_ref[5:6, cols]

    _run_tile_layer((w_in_hbm, w_out_hbm, w_br_a_hbm, w_br_b_hbm, w_br_c_hbm), x_ref, vd_ref, out_ref,
                    lhs_ref, p_ref, s1_ref, su_ref, m_ref, wbuf_ref, w2buf_ref, wsem, w2sem,
                    tm, {1: conv_branch, 4: gmlp_branch, 6: pool_branch})


def _weight_specs():
    return [pl.BlockSpec(memory_space=pl.ANY)] * 5 + [
        pl.BlockSpec((None, 2, D_MODEL), lambda i, l: (l, 0, 0)),
        pl.BlockSpec((None, 6, E), lambda i, l: (l, 0, 0)),
        pl.BlockSpec((None, CONV_W, E), lambda i, l: (l, 0, 0)),
    ]


def _work_buffers(tm):
    return [
        pltpu.VMEM((tm, D_MODEL), bf16),
        pltpu.VMEM((2, tm, COL), f32),
        pltpu.VMEM((tm, E), f32),
        pltpu.VMEM((N_BRANCH, tm, E), bf16),
        pltpu.VMEM((tm, D_MODEL), f32),
        pltpu.VMEM((N_SLOTS, D_MODEL, COL), bf16),
        pltpu.VMEM((N_SLOTS, E, COL), bf16),
        pltpu.SemaphoreType.DMA((N_SLOTS,)),
        pltpu.SemaphoreType.DMA((N_SLOTS,)),
    ]


def _compiler_params():
    return pltpu.CompilerParams(dimension_semantics=("arbitrary", "arbitrary"),
                                vmem_limit_bytes=VMEM_LIMIT_BYTES)


def _prompt_group(depth, x2d, n_seq, seq_len, weights, vd, ve, cw, ws, bs_full, pw):
    tm = TM_PROMPT
    assert seq_len % tm == 0 and tm % CHUNK == 0
    tiles_per_seq = seq_len // tm
    n_rows = n_seq * seq_len
    grid = (n_rows // tm, depth)
    in_specs = [pl.BlockSpec((tm, D_MODEL), lambda i, l: (i, 0))] + _weight_specs() + [
        pl.BlockSpec((None, B_GROUPS, CHUNK, CHUNK), lambda i, l: (l, 0, 0, 0)),
        pl.BlockSpec((None, CHUNK, E), lambda i, l: (l, 0, 0)),
        pl.BlockSpec((None, len(POOL_WINDOWS), C_GC, C_GC), lambda i, l: (l, 0, 0, 0)),
    ]
    out_specs = [
        pl.BlockSpec((tm, D_MODEL), lambda i, l: (i, 0)),
        pl.BlockSpec((None, None, CONV_W - 1, E), lambda i, l: (l, i, 0, 0)),
        pl.BlockSpec((None, None, POOL_MAX - 1, E), lambda i, l: (l, i, 0, 0)),
    ]
    out_shape = [
        jax.ShapeDtypeStruct((n_rows, D_MODEL), f32),
        jax.ShapeDtypeStruct((depth, n_rows // tm, CONV_W - 1, E), f32),
        jax.ShapeDtypeStruct((depth, n_rows // tm, POOL_MAX - 1, E), f32),
    ]
    scratch = _work_buffers(tm) + [
        pltpu.VMEM((CONV_HALO + tm, E), f32),
        pltpu.VMEM((POOL_HALO + tm, E), f32),
        pltpu.VMEM((depth, CONV_HALO, E), f32),
        pltpu.VMEM((depth, POOL_HALO, E), f32),
    ]
    x_out, conv_tiles, pool_tiles = pl.pallas_call(
        functools.partial(_prompt_kernel, tiles_per_seq=tiles_per_seq),
        grid=grid, in_specs=in_specs, out_specs=out_specs, out_shape=out_shape,
        scratch_shapes=scratch, compiler_params=_compiler_params(),
        name="prompt_group",
    )(x2d, *weights, vd, ve, cw, ws, bs_full, pw)
    last = slice(tiles_per_seq - 1, None, tiles_per_seq)
    return x_out, conv_tiles[:, last], pool_tiles[:, last]


def _sample_group(depth, x_sample, weights, vd, ve, cw, mix, b8, pw, state_conv, state_pool):
    tm = TM_SAMPLE
    n_dec = x_sample.shape[0]
    sq = tm // DEC_SEQ
    assert n_dec % sq == 0
    n_tiles = n_dec // sq
    x2d = jnp.transpose(x_sample.reshape(n_tiles, sq, DEC_SEQ, D_MODEL), (0, 2, 1, 3)).reshape(n_dec * DEC_SEQ, D_MODEL)
    stc = jnp.transpose(state_conv, (0, 2, 1, 3))
    stp = jnp.transpose(state_pool, (0, 2, 1, 3))
    grid = (n_tiles, depth)
    in_specs = [pl.BlockSpec((tm, D_MODEL), lambda i, l: (i, 0))] + _weight_specs() + [
        pl.BlockSpec((None, DEC_SEQ, DEC_SEQ, E), lambda i, l: (l, 0, 0, 0)),
        pl.BlockSpec((None, DEC_SEQ, E), lambda i, l: (l, 0, 0)),
        pl.BlockSpec((None, len(POOL_WINDOWS), C_GC, C_GC), lambda i, l: (l, 0, 0, 0)),
        pl.BlockSpec((None, CONV_W - 1, sq, E), lambda i, l: (l, 0, i, 0)),
        pl.BlockSpec((None, POOL_MAX - 1, sq, E), lambda i, l: (l, 0, i, 0)),
    ]
    out_specs = [
        pl.BlockSpec((tm, D_MODEL), lambda i, l: (i, 0)),
        pl.BlockSpec((None, CONV_W - 1, sq, E), lambda i, l: (l, 0, i, 0)),
        pl.BlockSpec((None, POOL_MAX - 1, sq, E), lambda i, l: (l, 0, i, 0)),
        pl.BlockSpec((None, DEC_SEQ, sq, E), lambda i, l: (l, 0, i, 0)),
    ]
    out_shape = [
        jax.ShapeDtypeStruct((n_dec * DEC_SEQ, D_MODEL), f32),
        jax.ShapeDtypeStruct((depth, CONV_W - 1, n_dec, E), f32),
        jax.ShapeDtypeStruct((depth, POOL_MAX - 1, n_dec, E), f32),
        jax.ShapeDtypeStruct((depth, DEC_SEQ, n_dec, E), f32),
    ]
    x_out, conv_s, pool_s, v_s = pl.pallas_call(
        _sample_kernel,
        grid=grid, in_specs=in_specs, out_specs=out_specs, out_shape=out_shape,
        scratch_shapes=_work_buffers(tm), compiler_params=_compiler_params(),
        name="sample_group",
    )(x2d, *weights, vd, ve, cw, mix, b8, pw, stc, stp)
    x_out = jnp.transpose(x_out.reshape(n_tiles, DEC_SEQ, sq, D_MODEL), (0, 2, 1, 3)).reshape(n_dec, DEC_SEQ, D_MODEL)
    back = lambda a: jnp.transpose(a, (0, 2, 1, 3))
    return x_out, back(conv_s), back(pool_s), back(v_s)


def _sample_mix_tables(gmlp_ws, gmlp_bs):
    mix = jnp.repeat(jnp.transpose(gmlp_ws[:, :, :DEC_SEQ, :DEC_SEQ], (0, 2, 3, 1)), B_GC, axis=-1)
    b8 = jnp.repeat(jnp.transpose(gmlp_bs[:, :, :DEC_SEQ], (0, 2, 1)), B_GC, axis=-1)
    return mix, b8


def kernel(x_prompt, x_sample, state_conv, state_pool, g_pre, w_in, conv_w, conv_b, conv_ln_g, conv_ln_b,
           w_br_a, gmlp_ln_g, gmlp_ln_b, gmlp_ws, gmlp_bs, w_br_b, pool_w, pool_scale, w_br_c, w_out, g_post):
    depth = w_in.shape[0]
    n_seq, seq_len, _ = x_prompt.shape
    n_dec, dec_seq, _ = x_sample.shape
    assert dec_seq == DEC_SEQ and x_prompt.shape[2] == D_MODEL

    weights = tuple(w.astype(bf16) for w in (w_in, w_out, w_br_a, w_br_b, w_br_c))
    vd = jnp.stack([g_pre, g_post], axis=1)
    ve = jnp.stack([conv_b, conv_ln_g, conv_ln_b, gmlp_ln_g, gmlp_ln_b, pool_scale], axis=1)
    bs_full = jnp.repeat(jnp.transpose(gmlp_bs, (0, 2, 1)), B_GC, axis=-1)
    pw = pool_w.astype(bf16)
    mix, b8 = _sample_mix_tables(gmlp_ws, gmlp_bs)

    xp, conv_p, pool_p = _prompt_group(depth, x_prompt.reshape(n_seq * seq_len, D_MODEL), n_seq, seq_len,
                                       weights, vd, ve, conv_w, gmlp_ws, bs_full, pw)
    xs, conv_s, pool_s, v_s = _sample_group(depth, x_sample, weights, vd, ve, conv_w, mix, b8, pw,
                                            state_conv, state_pool)
    return (xp.reshape(n_seq, seq_len, D_MODEL), xs, conv_p, pool_p, conv_s, pool_s, v_s)
```

```python
import functools

import jax
import jax.numpy as jnp
import numpy as np
from jax import lax
from jax.experimental import pallas as pl
from jax.experimental.pallas import tpu as pltpu

D_MODEL = 2048
E = 1024
N_BRANCH = 3
CONV_W = 31
CONV_HALO = 32
SUBLANES = 8
LANES = 128
CHUNK = 128
B_GROUPS = 8
B_GC = E // B_GROUPS
POOL_WINDOWS = (2, 4, 8, 16)
C_GC = E // len(POOL_WINDOWS)
POOL_MAX = 16
POOL_HALO = 32
DEC_SEQ = 8
PAST_LEN = 16384
EPS = 1e-6
COL = 1024
N_IN_BLOCKS = (8 * E + N_BRANCH * D_MODEL) // COL
N_BLOCKS = N_IN_BLOCKS + D_MODEL // COL
FIRST_MERGE_BLOCK = 8
N_MERGE = 2 * N_BRANCH
N_SLOTS = 2
N_DMA_PARTS = 2

TM_PROMPT = 512
TM_SAMPLE = 256
VMEM_LIMIT_BYTES = 60 * 1024 * 1024

f32 = jnp.float32
bf16 = jnp.bfloat16


def _for_rows(n_rows, rc, fn):
    def body(c, carry):
        fn(pl.multiple_of(c * rc, rc))
        return carry
    lax.fori_loop(0, n_rows // rc, body, 0)


def _sigmoid(x):
    return 0.5 * jnp.tanh(0.5 * x) + 0.5


def _layernorm_rows(y, g, b):
    mu = jnp.mean(y, axis=-1, keepdims=True)
    yc = y - mu
    var = jnp.mean(yc * yc, axis=-1, keepdims=True)
    return yc * lax.rsqrt(var + EPS) * g + b


def _start_layer(l, x_ref, vd_ref, out_ref, lhs_ref, tm):
    @pl.when(l == 0)
    def _():
        def load(r0):
            out_ref[pl.ds(r0, 128), :] = x_ref[pl.ds(r0, 128), :]
        _for_rows(tm, 128, load)

    def norm(r0):
        r = pl.ds(r0, 128)
        x = out_ref[r, :]
        ms = jnp.mean(x * x, axis=-1, keepdims=True)
        lhs_ref[r, :] = (x * lax.rsqrt(ms + EPS) * vd_ref[0:1, :]).astype(bf16)
    _for_rows(tm, 128, norm)


def _widen(p_ref, s1_ref, su_ref, tm):
    def go(r0):
        r = pl.ds(r0, 64)
        p = p_ref[r, :]
        su_ref[r, :] = (s1_ref[r, :] * (p * _sigmoid(p))).astype(bf16)
    _for_rows(tm, 64, go)


def _merge(p_ref, su_ref, w2_ref, m_ref, half, first):
    cols = slice(half * COL, (half + 1) * COL)
    y = jnp.dot(su_ref[...], w2_ref[...], preferred_element_type=f32)
    g = _sigmoid(p_ref[...]) * y
    if first:
        m_ref[:, cols] = g
    else:
        m_ref[:, cols] += g


def _m_to_lhs(m_ref, lhs_ref, tm):
    def go(r0):
        r = pl.ds(r0, 64)
        lhs_ref[r, :] = m_ref[r, :].astype(bf16)
    _for_rows(tm, 64, go)


def _post_norm_residual(y0_ref, y1_ref, vd_ref, x_ref, tm):
    def go(r0):
        r = pl.ds(r0, 128)
        y0 = y0_ref[r, :]
        y1 = y1_ref[r, :]
        ss = jnp.sum(y0 * y0, axis=-1, keepdims=True) + jnp.sum(y1 * y1, axis=-1, keepdims=True)
        inv = lax.rsqrt(ss / D_MODEL + EPS)
        x_ref[r, 0:COL] = x_ref[r, 0:COL] + y0 * inv * vd_ref[1:2, 0:COL]
        x_ref[r, COL:D_MODEL] = x_ref[r, COL:D_MODEL] + y1 * inv * vd_ref[1:2, COL:D_MODEL]
    _for_rows(tm, 128, go)


def _ln_rows(src_ref, dst_ref, g, b, tm, swish):
    rc = min(tm, 256)
    def go(r0):
        r = pl.ds(r0, rc)
        y = _layernorm_rows(src_ref[r, :], g, b)
        dst_ref[r, :] = y * _sigmoid(y) if swish else y
    _for_rows(tm, rc, go)


def _run_tile_layer(weights, x_ref, vd_ref, out_ref,
                    lhs_ref, p_ref, s1_ref, su_ref, m_ref, wbuf_ref, w2buf_ref, wsem, w2sem,
                    tm, branch_steps):
    w_in_hbm, w_out_hbm = weights[0], weights[1]
    w_br_hbm = weights[2:]
    i, l = pl.program_id(0), pl.program_id(1)
    n_tiles, depth = pl.num_programs(0), pl.num_programs(1)
    is_first = jnp.logical_and(i == 0, l == 0)
    is_last = jnp.logical_and(i == n_tiles - 1, l == depth - 1)
    next_l = jnp.where(l + 1 < depth, l + 1, 0)

    class _SplitCopy:
        def __init__(self, src, dst, sems):
            rows = src.shape[0] // N_DMA_PARTS
            self.parts = [pltpu.make_async_copy(src.at[pl.ds(h * rows, rows), :], dst.at[pl.ds(h * rows, rows), :],
                                                sems.at[h]) for h in range(N_DMA_PARTS)]

        def start(self):
            for h, part in enumerate(self.parts):
                part.start(priority=h)

        def wait(self):
            for part in self.parts:
                part.wait()

    def w_copy(layer, k):
        slot = k % N_SLOTS
        src, kk = (w_in_hbm, k) if k < N_IN_BLOCKS else (w_out_hbm, k - N_IN_BLOCKS)
        return _SplitCopy(src.at[layer, :, pl.ds(kk * COL, COL)], wbuf_ref.at[slot], wsem.at[slot])

    def w2_copy(j):
        slot = j % N_SLOTS
        return _SplitCopy(w_br_hbm[j // 2].at[l, :, pl.ds((j % 2) * COL, COL)], w2buf_ref.at[slot], w2sem.at[slot])

    @pl.when(is_first)
    def _():
        w_copy(l, 0).start()

    _start_layer(l, x_ref, vd_ref, out_ref, lhs_ref, tm)

    for k in range(N_BLOCKS):
        w_copy(l, k).wait()
        if k + 1 < N_BLOCKS:
            w_copy(l, k + 1).start()
        else:
            @pl.when(jnp.logical_not(is_last))
            def _():
                w_copy(next_l, 0).start()
        if k == FIRST_MERGE_BLOCK - 1:
            w2_copy(0).start()

        p_ref[k % 2] = jnp.dot(lhs_ref[...], wbuf_ref[k % N_SLOTS], preferred_element_type=f32)

        if k in branch_steps:
            branch_steps[k]()
        if k in (2, 5, 7):
            _widen(p_ref.at[k % 2], s1_ref, su_ref.at[(2, 5, 7).index(k)], tm)
        if FIRST_MERGE_BLOCK <= k < N_IN_BLOCKS:
            j = k - FIRST_MERGE_BLOCK
            w2_copy(j).wait()
            if j + 1 < N_MERGE:
                w2_copy(j + 1).start()
            _merge(p_ref.at[k % 2], su_ref.at[j // 2], w2buf_ref.at[j % N_SLOTS], m_ref, j % 2, first=j < 2)
        if k == N_IN_BLOCKS - 1:
            _m_to_lhs(m_ref, lhs_ref, tm)
        if k == N_BLOCKS - 1:
            _post_norm_residual(p_ref.at[0], p_ref.at[1], vd_ref, out_ref, tm)


def _conv_rows_prompt(zbuf_ref, cw_ref, ve_ref, dst_ref, r0, rc):
    for cb in range(E // LANES):
        cols = slice(cb * LANES, (cb + 1) * LANES)
        blk = zbuf_ref[pl.ds(r0, rc + CONV_HALO), cols]
        lead = CONV_HALO - (CONV_W - 1)
        y = ve_ref[0:1, cols] + cw_ref[CONV_W - 1:CONV_W, cols] * blk[CONV_HALO:CONV_HALO + rc, :]
        for b in range(SUBLANES):
            acc = None
            for a in range(CONV_HALO // SUBLANES):
                k = SUBLANES * a + b - lead
                if k < 0:
                    continue
                term = cw_ref[k:k + 1, cols] * blk[SUBLANES * a:SUBLANES * a + rc + SUBLANES, :]
                acc = term if acc is None else acc + term
            y = y + acc[b:b + rc, :]
        dst_ref[pl.ds(r0, rc), cols] = y


def _trailing_sum(blk, w):
    rows = blk.shape[0]
    win, lo, span = blk, 0, 1
    while span < w:
        n = rows - (lo + SUBLANES)
        win = win[SUBLANES:SUBLANES + n, :] + win[SUBLANES - span:SUBLANES - span + n, :]
        lo += SUBLANES
        span *= 2
    return win[POOL_HALO - lo:, :]


def _prompt_kernel(x_ref, w_in_hbm, w_out_hbm, w_br_a_hbm, w_br_b_hbm, w_br_c_hbm,
                   vd_ref, ve_ref, cw_ref, ws_ref, bs_ref, pw_ref,
                   out_ref, cst_ref, pst_ref,
                   lhs_ref, p_ref, s1_ref, su_ref, m_ref, wbuf_ref, w2buf_ref, wsem, w2sem,
                   zbuf_ref, cbuf_ref, zhalo_ref, chalo_ref,
                   *, tiles_per_seq):
    tm = TM_PROMPT
    l = pl.program_id(1)
    q = pl.program_id(0) % tiles_per_seq

    def conv_branch():
        @pl.when(q == 0)
        def _():
            zbuf_ref[0:CONV_HALO, :] = jnp.zeros((CONV_HALO, E), f32)

        @pl.when(q > 0)
        def _():
            zbuf_ref[0:CONV_HALO, :] = zhalo_ref[l]

        def glu(r0):
            r = pl.ds(r0, 64)
            zbuf_ref[pl.ds(CONV_HALO + r0, 64), :] = p_ref[0, r, :] * _sigmoid(p_ref[1, r, :])
        _for_rows(tm, 64, glu)

        _for_rows(tm, CHUNK, lambda r0: _conv_rows_prompt(zbuf_ref, cw_ref, ve_ref, s1_ref, r0, CHUNK))
        _ln_rows(s1_ref, s1_ref, ve_ref[1:2, :], ve_ref[2:3, :], tm, swish=True)

        cst_ref[...] = zbuf_ref[CONV_HALO + tm - (CONV_W - 1):CONV_HALO + tm, :]
        zhalo_ref[l] = zbuf_ref[tm:tm + CONV_HALO, :]

    def gmlp_branch():
        _ln_rows(p_ref.at[0], s1_ref, ve_ref[3:4, :], ve_ref[4:5, :], tm, swish=False)
        tril = (lax.broadcasted_iota(jnp.int32, (CHUNK, CHUNK), 0)
                >= lax.broadcasted_iota(jnp.int32, (CHUNK, CHUNK), 1))
        for grp in range(B_GROUPS):
            cols = slice(grp * B_GC, (grp + 1) * B_GC)
            w_g = jnp.where(tril, ws_ref[grp], 0.0).astype(bf16)
            for c in range(tm // CHUNK):
                rows = slice(c * CHUNK, (c + 1) * CHUNK)
                v = s1_ref[rows, cols].astype(bf16)
                mixed = jnp.dot(w_g, v, preferred_element_type=f32) + bs_ref[:, cols]
                s1_ref[rows, cols] = p_ref[1, rows, cols] * mixed

    def pool_branch():
        @pl.when(q == 0)
        def _():
            cbuf_ref[0:POOL_HALO, :] = jnp.zeros((POOL_HALO, E), f32)

        @pl.when(q > 0)
        def _():
            cbuf_ref[0:POOL_HALO, :] = chalo_ref[l]

        def stash(r0):
            cbuf_ref[pl.ds(POOL_HALO + r0, 64), :] = p_ref[0, pl.ds(r0, 64), :]
        _for_rows(tm, 64, stash)

        def pool(r0):
            pos = q * tm + r0 + lax.broadcasted_iota(jnp.int32, (CHUNK, C_GC), 0)
            for gi, w in enumerate(POOL_WINDOWS):
                cols = slice(gi * C_GC, (gi + 1) * C_GC)
                blk = cbuf_ref[pl.ds(r0, POOL_HALO + CHUNK), cols]
                c = blk[POOL_HALO:, :]
                cnt = jnp.minimum(w, pos + 1).astype(f32)
                d = _trailing_sum(blk, w) / cnt - c
                y = jnp.dot(d.astype(bf16), pw_ref[gi], preferred_element_type=f32)
                s1_ref[pl.ds(r0, CHUNK), cols] = y * ve_ref[5:6, cols]
        _for_rows(tm, CHUNK, pool)

        pst_ref[...] = cbuf_ref[POOL_HALO + tm - (POOL_MAX - 1):POOL_HALO + tm, :]
        chalo_ref[l] = cbuf_ref[tm:tm + POOL_HALO, :]

    _run_tile_layer((w_in_hbm, w_out_hbm, w_br_a_hbm, w_br_b_hbm, w_br_c_hbm), x_ref, vd_ref, out_ref,
                    lhs_ref, p_ref, s1_ref, su_ref, m_ref, wbuf_ref, w2buf_ref, wsem, w2sem,
                    tm, {1: conv_branch, 4: gmlp_branch, 6: pool_branch})


def _sample_kernel(x_ref, w_in_hbm, w_out_hbm, w_br_a_hbm, w_br_b_hbm, w_br_c_hbm,
                   vd_ref, ve_ref, cw_ref, mix_ref, b8_ref, pw_ref, stc_ref, stp_ref,
                   out_ref, cso_ref, pso_ref, v_ref,
                   lhs_ref, p_ref, s1_ref, su_ref, m_ref, wbuf_ref, w2buf_ref, wsem, w2sem):
    tm = TM_SAMPLE
    sq = tm // DEC_SEQ
    cbw = 256

    def slab(t):
        return slice(t * sq, (t + 1) * sq)

    def conv_branch():
        def glu(r0):
            r = pl.ds(r0, 64)
            m_ref[r, 0:E] = p_ref[0, r, :] * _sigmoid(p_ref[1, r, :])
        _for_rows(tm, 64, glu)

        def zp(j, cols):
            if j < CONV_W - 1:
                return stc_ref[j, :, cols]
            return m_ref[slab(j - (CONV_W - 1)), cols]

        for j in range(CONV_W - 1):
            cso_ref[j] = zp(j + DEC_SEQ, slice(0, E))
        for cb in range(E // cbw):
            cols = slice(cb * cbw, (cb + 1) * cbw)
            for t in range(DEC_SEQ):
                acc = jnp.broadcast_to(ve_ref[0:1, cols], (sq, cbw))
                for k in range(CONV_W):
                    acc = acc + cw_ref[k:k + 1, cols] * zp(t + k, cols)
                s1_ref[slab(t), cols] = acc
        _ln_rows(s1_ref, s1_ref, ve_ref[1:2, :], ve_ref[2:3, :], tm, swish=True)

    def gmlp_branch():
        _ln_rows(p_ref.at[0], m_ref.at[:, 0:E], ve_ref[3:4, :], ve_ref[4:5, :], tm, swish=False)
        for t in range(DEC_SEQ):
            v_ref[t] = m_ref[slab(t), 0:E]
        for cb in range(E // cbw):
            cols = slice(cb * cbw, (cb + 1) * cbw)
            for t in range(DEC_SEQ):
                acc = jnp.broadcast_to(b8_ref[t:t + 1, cols], (sq, cbw))
                for s in range(t + 1):
                    acc = acc + mix_ref[t, s:s + 1, cols] * m_ref[slab(s), cols]
                s1_ref[slab(t), cols] = p_ref[1, slab(t), cols] * acc

    def pool_branch():
        def zc(j, cols):
            if j < POOL_MAX - 1:
                return stp_ref[j, :, cols]
            return p_ref[0, slab(j - (POOL_MAX - 1)), cols]

        for j in range(POOL_MAX - 1):
            pso_ref[j] = zc(j + DEC_SEQ, slice(0, E))
        for gi, w in enumerate(POOL_WINDOWS):
            cols = slice(gi * C_GC, (gi + 1) * C_GC)
            for t in range(DEC_SEQ):
                c = zc(POOL_MAX - 1 + t, cols)
                win = c
                for j in range(1, w):
                    win = win + zc(POOL_MAX - 1 + t - j, cols)
                s1_ref[slab(t), cols] = win / float(min(w, PAST_LEN + 1)) - c
        for gi in range(len(POOL_WINDOWS)):
            cols = slice(gi * C_GC, (gi + 1) * C_GC)
            y = jnp.dot(s1_ref[:, cols].astype(bf16), pw_ref[gi], preferred_element_type=f32)
            s1_ref[:, cols] = y * ve_ref[5:6, cols]

    _run_tile_layer((w_in_hbm, w_out_hbm, w_br_a_hbm, w_br_b_hbm, w_br_c_hbm), x_ref, vd_ref, out_ref,
                    lhs_ref, p_ref, s1_ref, su_ref, m_ref, wbuf_ref, w2buf_ref, wsem, w2sem,
                    tm, {1: conv_branch, 4: gmlp_branch, 6: pool_branch})


def _weight_specs():
    return [pl.BlockSpec(memory_space=pl.ANY)] * 5 + [
        pl.BlockSpec((None, 2, D_MODEL), lambda i, l: (l, 0, 0)),
        pl.BlockSpec((None, 6, E), lambda i, l: (l, 0, 0)),
        pl.BlockSpec((None, CONV_W, E), lambda i, l: (l, 0, 0)),
    ]


def _work_buffers(tm):
    return [
        pltpu.VMEM((tm, D_MODEL), bf16),
        pltpu.VMEM((2, tm, COL), f32),
        pltpu.VMEM((tm, E), f32),
        pltpu.VMEM((N_BRANCH, tm, E), bf16),
        pltpu.VMEM((tm, D_MODEL), f32),
        pltpu.VMEM((N_SLOTS, D_MODEL, COL), bf16),
        pltpu.VMEM((N_SLOTS, E, COL), bf16),
        pltpu.SemaphoreType.DMA((N_SLOTS, N_DMA_PARTS)),
        pltpu.SemaphoreType.DMA((N_SLOTS, N_DMA_PARTS)),
    ]


def _compiler_params():
    return pltpu.CompilerParams(dimension_semantics=("arbitrary", "arbitrary"),
                                vmem_limit_bytes=VMEM_LIMIT_BYTES)


def _prompt_group(depth, x2d, n_seq, seq_len, weights, vd, ve, cw, ws, bs_full, pw):
    tm = TM_PROMPT
    assert seq_len % tm == 0 and tm % CHUNK == 0
    tiles_per_seq = seq_len // tm
    n_rows = n_seq * seq_len
    grid = (n_rows // tm, depth)
    in_specs = [pl.BlockSpec((tm, D_MODEL), lambda i, l: (i, 0))] + _weight_specs() + [
        pl.BlockSpec((None, B_GROUPS, CHUNK, CHUNK), lambda i, l: (l, 0, 0, 0)),
        pl.BlockSpec((None, CHUNK, E), lambda i, l: (l, 0, 0)),
        pl.BlockSpec((None, len(POOL_WINDOWS), C_GC, C_GC), lambda i, l: (l, 0, 0, 0)),
    ]
    out_specs = [
        pl.BlockSpec((tm, D_MODEL), lambda i, l: (i, 0)),
        pl.BlockSpec((None, None, CONV_W - 1, E), lambda i, l: (l, i, 0, 0)),
        pl.BlockSpec((None, None, POOL_MAX - 1, E), lambda i, l: (l, i, 0, 0)),
    ]
    out_shape = [
        jax.ShapeDtypeStruct((n_rows, D_MODEL), f32),
        jax.ShapeDtypeStruct((depth, n_rows // tm, CONV_W - 1, E), f32),
        jax.ShapeDtypeStruct((depth, n_rows // tm, POOL_MAX - 1, E), f32),
    ]
    scratch = _work_buffers(tm) + [
        pltpu.VMEM((CONV_HALO + tm, E), f32),
        pltpu.VMEM((POOL_HALO + tm, E), f32),
        pltpu.VMEM((depth, CONV_HALO, E), f32),
        pltpu.VMEM((depth, POOL_HALO, E), f32),
    ]
    x_out, conv_tiles, pool_tiles = pl.pallas_call(
        functools.partial(_prompt_kernel, tiles_per_seq=tiles_per_seq),
        grid=grid, in_specs=in_specs, out_specs=out_specs, out_shape=out_shape,
        scratch_shapes=scratch, compiler_params=_compiler_params(),
        name="prompt_group",
    )(x2d, *weights, vd, ve, cw, ws, bs_full, pw)
    last = slice(tiles_per_seq - 1, None, tiles_per_seq)
    return x_out, conv_tiles[:, last], pool_tiles[:, last]


def _sample_group(depth, x_sample, weights, vd, ve, cw, mix, b8, pw, state_conv, state_pool):
    tm = TM_SAMPLE
    n_dec = x_sample.shape[0]
    sq = tm // DEC_SEQ
    assert n_dec % sq == 0
    n_tiles = n_dec // sq
    x2d = jnp.transpose(x_sample.reshape(n_tiles, sq, DEC_SEQ, D_MODEL), (0, 2, 1, 3)).reshape(n_dec * DEC_SEQ, D_MODEL)
    stc = jnp.transpose(state_conv, (0, 2, 1, 3))
    stp = jnp.transpose(state_pool, (0, 2, 1, 3))
    grid = (n_tiles, depth)
    in_specs = [pl.BlockSpec((tm, D_MODEL), lambda i, l: (i, 0))] + _weight_specs() + [
        pl.BlockSpec((None, DEC_SEQ, DEC_SEQ, E), lambda i, l: (l, 0, 0, 0)),
        pl.BlockSpec((None, DEC_SEQ, E), lambda i, l: (l, 0, 0)),
        pl.BlockSpec((None, len(POOL_WINDOWS), C_GC, C_GC), lambda i, l: (l, 0, 0, 0)),
        pl.BlockSpec((None, CONV_W - 1, sq, E), lambda i, l: (l, 0, i, 0)),
        pl.BlockSpec((None, POOL_MAX - 1, sq, E), lambda i, l: (l, 0, i, 0)),
    ]
    out_specs = [
        pl.BlockSpec((tm, D_MODEL), lambda i, l: (i, 0)),
        pl.BlockSpec((None, CONV_W - 1, sq, E), lambda i, l: (l, 0, i, 0)),
        pl.BlockSpec((None, POOL_MAX - 1, sq, E), lambda i, l: (l, 0, i, 0)),
        pl.BlockSpec((None, DEC_SEQ, sq, E), lambda i, l: (l, 0, i, 0)),
    ]
    out_shape = [
        jax.ShapeDtypeStruct((n_dec * DEC_SEQ, D_MODEL), f32),
        jax.ShapeDtypeStruct((depth, CONV_W - 1, n_dec, E), f32),
        jax.ShapeDtypeStruct((depth, POOL_MAX - 1, n_dec, E), f32),
        jax.ShapeDtypeStruct((depth, DEC_SEQ, n_dec, E), f32),
    ]
    x_out, conv_s, pool_s, v_s = pl.pallas_call(
        _sample_kernel,
        grid=grid, in_specs=in_specs, out_specs=out_specs, out_shape=out_shape,
        scratch_shapes=_work_buffers(tm), compiler_params=_compiler_params(),
        name="sample_group",
    )(x2d, *weights, vd, ve, cw, mix, b8, pw, stc, stp)
    x_out = jnp.transpose(x_out.reshape(n_tiles, DEC_SEQ, sq, D_MODEL), (0, 2, 1, 3)).reshape(n_dec, DEC_SEQ, D_MODEL)
    back = lambda a: jnp.transpose(a, (0, 2, 1, 3))
    return x_out, back(conv_s), back(pool_s), back(v_s)


def _sample_mix_tables(gmlp_ws, gmlp_bs):
    mix = jnp.repeat(jnp.transpose(gmlp_ws[:, :, :DEC_SEQ, :DEC_SEQ], (0, 2, 3, 1)), B_GC, axis=-1)
    b8 = jnp.repeat(jnp.transpose(gmlp_bs[:, :, :DEC_SEQ], (0, 2, 1)), B_GC, axis=-1)
    return mix, b8


def kernel(x_prompt, x_sample, state_conv, state_pool, g_pre, w_in, conv_w, conv_b, conv_ln_g, conv_ln_b,
           w_br_a, gmlp_ln_g, gmlp_ln_b, gmlp_ws, gmlp_bs, w_br_b, pool_w, pool_scale, w_br_c, w_out, g_post):
    depth = w_in.shape[0]
    n_seq, seq_len, _ = x_prompt.shape
    n_dec, dec_seq, _ = x_sample.shape
    assert dec_seq == DEC_SEQ and x_prompt.shape[2] == D_MODEL

    weights = tuple(w.astype(bf16) for w in (w_in, w_out, w_br_a, w_br_b, w_br_c))
    vd = jnp.stack([g_pre, g_post], axis=1)
    ve = jnp.stack([conv_b, conv_ln_g, conv_ln_b, gmlp_ln_g, gmlp_ln_b, pool_scale], axis=1)
    bs_full = jnp.repeat(jnp.transpose(gmlp_bs, (0, 2, 1)), B_GC, axis=-1)
    pw = pool_w.astype(bf16)
    mix, b8 = _sample_mix_tables(gmlp_ws, gmlp_bs)

    xp, conv_p, pool_p = _prompt_group(depth, x_prompt.reshape(n_seq * seq_len, D_MODEL), n_seq, seq_len,
                                       weights, vd, ve, conv_w, gmlp_ws, bs_full, pw)
    xs, conv_s, pool_s, v_s = _sample_group(depth, x_sample, weights, vd, ve, conv_w, mix, b8, pw,
                                            state_conv, state_pool)
    return (xp.reshape(n_seq, seq_len, D_MODEL), xs, conv_p, pool_p, conv_s, pool_s, v_s)
```

```python
import functools

import jax
import jax.numpy as jnp
import numpy as np
from jax import lax
from jax.experimental import pallas as pl
from jax.experimental.pallas import tpu as pltpu

D_MODEL = 2048
E = 1024
N_BRANCH = 3
CONV_W = 31
CONV_HALO = 32
SUBLANES = 8
LANES = 128
CHUNK = 128
B_GROUPS = 8
B_GC = E // B_GROUPS
POOL_WINDOWS = (2, 4, 8, 16)
C_GC = E // len(POOL_WINDOWS)
POOL_MAX = 16
POOL_HALO = 32
DEC_SEQ = 8
PAST_LEN = 16384
EPS = 1e-6
COL = 1024
N_IN_BLOCKS = (8 * E + N_BRANCH * D_MODEL) // COL
N_BLOCKS = N_IN_BLOCKS + D_MODEL // COL
FIRST_MERGE_BLOCK = 8
N_MERGE = 2 * N_BRANCH
LOOKAHEAD = 2
N_SLOTS = LOOKAHEAD + 1
N_W2_SLOTS = 2

TM_PROMPT = 512
TM_SAMPLE = 256
VMEM_LIMIT_BYTES = 60 * 1024 * 1024

f32 = jnp.float32
bf16 = jnp.bfloat16


def _for_rows(n_rows, rc, fn):
    def body(c, carry):
        fn(pl.multiple_of(c * rc, rc))
        return carry
    lax.fori_loop(0, n_rows // rc, body, 0)


def _sigmoid(x):
    return 0.5 * jnp.tanh(0.5 * x) + 0.5


def _layernorm_rows(y, g, b):
    mu = jnp.mean(y, axis=-1, keepdims=True)
    yc = y - mu
    var = jnp.mean(yc * yc, axis=-1, keepdims=True)
    return yc * lax.rsqrt(var + EPS) * g + b


def _start_layer(l, x_ref, vd_ref, out_ref, lhs_ref, tm):
    @pl.when(l == 0)
    def _():
        def load(r0):
            out_ref[pl.ds(r0, 128), :] = x_ref[pl.ds(r0, 128), :]
        _for_rows(tm, 128, load)

    def norm(r0):
        r = pl.ds(r0, 128)
        x = out_ref[r, :]
        ms = jnp.mean(x * x, axis=-1, keepdims=True)
        lhs_ref[r, :] = (x * lax.rsqrt(ms + EPS) * vd_ref[0:1, :]).astype(bf16)
    _for_rows(tm, 128, norm)


def _widen(p_ref, s1_ref, su_ref, tm):
    def go(r0):
        r = pl.ds(r0, 64)
        p = p_ref[r, :]
        su_ref[r, :] = (s1_ref[r, :] * (p * _sigmoid(p))).astype(bf16)
    _for_rows(tm, 64, go)


def _merge(p_ref, su_ref, w2_ref, m_ref, half, first):
    cols = slice(half * COL, (half + 1) * COL)
    y = jnp.dot(su_ref[...], w2_ref[...], preferred_element_type=f32)
    g = _sigmoid(p_ref[...]) * y
    if first:
        m_ref[:, cols] = g
    else:
        m_ref[:, cols] += g


def _m_to_lhs(m_ref, lhs_ref, tm):
    def go(r0):
        r = pl.ds(r0, 64)
        lhs_ref[r, :] = m_ref[r, :].astype(bf16)
    _for_rows(tm, 64, go)


def _post_norm_residual(y0_ref, y1_ref, vd_ref, x_ref, tm):
    def go(r0):
        r = pl.ds(r0, 128)
        y0 = y0_ref[r, :]
        y1 = y1_ref[r, :]
        ss = jnp.sum(y0 * y0, axis=-1, keepdims=True) + jnp.sum(y1 * y1, axis=-1, keepdims=True)
        inv = lax.rsqrt(ss / D_MODEL + EPS)
        x_ref[r, 0:COL] = x_ref[r, 0:COL] + y0 * inv * vd_ref[1:2, 0:COL]
        x_ref[r, COL:D_MODEL] = x_ref[r, COL:D_MODEL] + y1 * inv * vd_ref[1:2, COL:D_MODEL]
    _for_rows(tm, 128, go)


def _ln_rows(src_ref, dst_ref, g, b, tm, swish):
    rc = min(tm, 256)
    def go(r0):
        r = pl.ds(r0, rc)
        y = _layernorm_rows(src_ref[r, :], g, b)
        dst_ref[r, :] = y * _sigmoid(y) if swish else y
    _for_rows(tm, rc, go)


def _run_tile_layer(weights, x_ref, vd_ref, out_ref,
                    lhs_ref, p_ref, s1_ref, su_ref, m_ref, wbuf_ref, w2buf_ref, wsem, w2sem,
                    tm, branch_steps):
    w_in_hbm, w_out_hbm = weights[0], weights[1]
    w_br_hbm = weights[2:]
    i, l = pl.program_id(0), pl.program_id(1)
    n_tiles, depth = pl.num_programs(0), pl.num_programs(1)
    is_first = jnp.logical_and(i == 0, l == 0)
    is_last = jnp.logical_and(i == n_tiles - 1, l == depth - 1)
    next_l = jnp.where(l + 1 < depth, l + 1, 0)

    ring_base = lax.rem((i * depth + l) * N_BLOCKS, N_SLOTS)

    def slot_of(k):
        return lax.rem(ring_base + k, N_SLOTS)

    def w_copy(layer, k):
        slot = slot_of(k)
        kk = k % N_BLOCKS
        src, kk = (w_in_hbm, kk) if kk < N_IN_BLOCKS else (w_out_hbm, kk - N_IN_BLOCKS)
        return pltpu.make_async_copy(src.at[layer, :, pl.ds(kk * COL, COL)], wbuf_ref.at[slot], wsem.at[slot])

    def w2_copy(j):
        slot = j % N_W2_SLOTS
        return pltpu.make_async_copy(w_br_hbm[j // 2].at[l, :, pl.ds((j % 2) * COL, COL)],
                                     w2buf_ref.at[slot], w2sem.at[slot])

    @pl.when(is_first)
    def _():
        for k in range(LOOKAHEAD):
            w_copy(l, k).start()

    _start_layer(l, x_ref, vd_ref, out_ref, lhs_ref, tm)

    for k in range(N_BLOCKS):
        w_copy(l, k).wait()
        if k + LOOKAHEAD < N_BLOCKS:
            w_copy(l, k + LOOKAHEAD).start()
        else:
            @pl.when(jnp.logical_not(is_last))
            def _(k=k):
                w_copy(next_l, k + LOOKAHEAD).start()
        if k == FIRST_MERGE_BLOCK - 1:
            w2_copy(0).start()

        p_ref[k % 2] = jnp.dot(lhs_ref[...], wbuf_ref[slot_of(k)], preferred_element_type=f32)

        if k in branch_steps:
            branch_steps[k]()
        if k in (2, 5, 7):
            _widen(p_ref.at[k % 2], s1_ref, su_ref.at[(2, 5, 7).index(k)], tm)
        if FIRST_MERGE_BLOCK <= k < N_IN_BLOCKS:
            j = k - FIRST_MERGE_BLOCK
            w2_copy(j).wait()
            if j + 1 < N_MERGE:
                w2_copy(j + 1).start()
            _merge(p_ref.at[k % 2], su_ref.at[j // 2], w2buf_ref.at[j % N_W2_SLOTS], m_ref, j % 2, first=j < 2)
        if k == N_IN_BLOCKS - 1:
            _m_to_lhs(m_ref, lhs_ref, tm)
        if k == N_BLOCKS - 1:
            _post_norm_residual(p_ref.at[0], p_ref.at[1], vd_ref, out_ref, tm)


def _conv_rows_prompt(zbuf_ref, cw_ref, ve_ref, dst_ref, r0, rc):
    for cb in range(E // LANES):
        cols = slice(cb * LANES, (cb + 1) * LANES)
        blk = zbuf_ref[pl.ds(r0, rc + CONV_HALO), cols]
        lead = CONV_HALO - (CONV_W - 1)
        y = ve_ref[0:1, cols] + cw_ref[CONV_W - 1:CONV_W, cols] * blk[CONV_HALO:CONV_HALO + rc, :]
        for b in range(SUBLANES):
            acc = None
            for a in range(CONV_HALO // SUBLANES):
                k = SUBLANES * a + b - lead
                if k < 0:
                    continue
                term = cw_ref[k:k + 1, cols] * blk[SUBLANES * a:SUBLANES * a + rc + SUBLANES, :]
                acc = term if acc is None else acc + term
            y = y + acc[b:b + rc, :]
        dst_ref[pl.ds(r0, rc), cols] = y


def _trailing_sum(blk, w):
    rows = blk.shape[0]
    win, lo, span = blk, 0, 1
    while span < w:
        n = rows - (lo + SUBLANES)
        win = win[SUBLANES:SUBLANES + n, :] + win[SUBLANES - span:SUBLANES - span + n, :]
        lo += SUBLANES
        span *= 2
    return win[POOL_HALO - lo:, :]


def _prompt_kernel(x_ref, w_in_hbm, w_out_hbm, w_br_a_hbm, w_br_b_hbm, w_br_c_hbm,
                   vd_ref, ve_ref, cw_ref, ws_ref, bs_ref, pw_ref,
                   out_ref, cst_ref, pst_ref,
                   lhs_ref, p_ref, s1_ref, su_ref, m_ref, wbuf_ref, w2buf_ref, wsem, w2sem,
                   zbuf_ref, cbuf_ref, zhalo_ref, chalo_ref,
                   *, tiles_per_seq):
    tm = TM_PROMPT
    l = pl.program_id(1)
    q = pl.program_id(0) % tiles_per_seq

    def conv_branch():
        @pl.when(q == 0)
        def _():
            zbuf_ref[0:CONV_HALO, :] = jnp.zeros((CONV_HALO, E), f32)

        @pl.when(q > 0)
        def _():
            zbuf_ref[0:CONV_HALO, :] = zhalo_ref[l]

        def glu(r0):
            r = pl.ds(r0, 64)
            zbuf_ref[pl.ds(CONV_HALO + r0, 64), :] = p_ref[0, r, :] * _sigmoid(p_ref[1, r, :])
        _for_rows(tm, 64, glu)

        _for_rows(tm, CHUNK, lambda r0: _conv_rows_prompt(zbuf_ref, cw_ref, ve_ref, s1_ref, r0, CHUNK))
        _ln_rows(s1_ref, s1_ref, ve_ref[1:2, :], ve_ref[2:3, :], tm, swish=True)

        cst_ref[...] = zbuf_ref[CONV_HALO + tm - (CONV_W - 1):CONV_HALO + tm, :]
        zhalo_ref[l] = zbuf_ref[tm:tm + CONV_HALO, :]

    def gmlp_branch():
        _ln_rows(p_ref.at[0], s1_ref, ve_ref[3:4, :], ve_ref[4:5, :], tm, swish=False)
        tril = (lax.broadcasted_iota(jnp.int32, (CHUNK, CHUNK), 0)
                >= lax.broadcasted_iota(jnp.int32, (CHUNK, CHUNK), 1))
        for grp in range(B_GROUPS):
            cols = slice(grp * B_GC, (grp + 1) * B_GC)
            w_g = jnp.where(tril, ws_ref[grp], 0.0).astype(bf16)
            for c in range(tm // CHUNK):
                rows = slice(c * CHUNK, (c + 1) * CHUNK)
                v = s1_ref[rows, cols].astype(bf16)
                mixed = jnp.dot(w_g, v, preferred_element_type=f32) + bs_ref[:, cols]
                s1_ref[rows, cols] = p_ref[1, rows, cols] * mixed

    def pool_branch():
        @pl.when(q == 0)
        def _():
            cbuf_ref[0:POOL_HALO, :] = jnp.zeros((POOL_HALO, E), f32)

        @pl.when(q > 0)
        def _():
            cbuf_ref[0:POOL_HALO, :] = chalo_ref[l]

        def stash(r0):
            cbuf_ref[pl.ds(POOL_HALO + r0, 64), :] = p_ref[0, pl.ds(r0, 64), :]
        _for_rows(tm, 64, stash)

        def pool(r0):
            pos = q * tm + r0 + lax.broadcasted_iota(jnp.int32, (CHUNK, C_GC), 0)
            for gi, w in enumerate(POOL_WINDOWS):
                cols = slice(gi * C_GC, (gi + 1) * C_GC)
                blk = cbuf_ref[pl.ds(r0, POOL_HALO + CHUNK), cols]
                c = blk[POOL_HALO:, :]
                cnt = jnp.minimum(w, pos + 1).astype(f32)
                d = _trailing_sum(blk, w) / cnt - c
                y = jnp.dot(d.astype(bf16), pw_ref[gi], preferred_element_type=f32)
                s1_ref[pl.ds(r0, CHUNK), cols] = y * ve_ref[5:6, cols]
        _for_rows(tm, CHUNK, pool)

        pst_ref[...] = cbuf_ref[POOL_HALO + tm - (POOL_MAX - 1):POOL_HALO + tm, :]
        chalo_ref[l] = cbuf_ref[tm:tm + POOL_HALO, :]

    _run_tile_layer((w_in_hbm, w_out_hbm, w_br_a_hbm, w_br_b_hbm, w_br_c_hbm), x_ref, vd_ref, out_ref,
                    lhs_ref, p_ref, s1_ref, su_ref, m_ref, wbuf_ref, w2buf_ref, wsem, w2sem,
                    tm, {1: conv_branch, 4: gmlp_branch, 6: pool_branch})


def _sample_kernel(x_ref, w_in_hbm, w_out_hbm, w_br_a_hbm, w_br_b_hbm, w_br_c_hbm,
                   vd_ref, ve_ref, cw_ref, mix_ref, b8_ref, pw_ref, stc_ref, stp_ref,
                   out_ref, cso_ref, pso_ref, v_ref,
                   lhs_ref, p_ref, s1_ref, su_ref, m_ref, wbuf_ref, w2buf_ref, wsem, w2sem):
    tm = TM_SAMPLE
    sq = tm // DEC_SEQ
    cbw = 256

    def slab(t):
        return slice(t * sq, (t + 1) * sq)

    def conv_branch():
        def glu(r0):
            r = pl.ds(r0, 64)
            m_ref[r, 0:E] = p_ref[0, r, :] * _sigmoid(p_ref[1, r, :])
        _for_rows(tm, 64, glu)

        def zp(j, cols):
            if j < CONV_W - 1:
                return stc_ref[j, :, cols]
            return m_ref[slab(j - (CONV_W - 1)), cols]

        for j in range(CONV_W - 1):
            cso_ref[j] = zp(j + DEC_SEQ, slice(0, E))
        for cb in range(E // cbw):
            cols = slice(cb * cbw, (cb + 1) * cbw)
            for t in range(DEC_SEQ):
                acc = jnp.broadcast_to(ve_ref[0:1, cols], (sq, cbw))
                for k in range(CONV_W):
                    acc = acc + cw_ref[k:k + 1, cols] * zp(t + k, cols)
                s1_ref[slab(t), cols] = acc
        _ln_rows(s1_ref, s1_ref, ve_ref[1:2, :], ve_ref[2:3, :], tm, swish=True)

    def gmlp_branch():
        _ln_rows(p_ref.at[0], m_ref.at[:, 0:E], ve_ref[3:4, :], ve_ref[4:5, :], tm, swish=False)
        for t in range(DEC_SEQ):
            v_ref[t] = m_ref[slab(t), 0:E]
        for cb in range(E // cbw):
            cols = slice(cb * cbw, (cb + 1) * cbw)
            for t in range(DEC_SEQ):
                acc = jnp.broadcast_to(b8_ref[t:t + 1, cols], (sq, cbw))
                for s in range(t + 1):
                    acc = acc + mix_ref[t, s:s + 1, cols] * m_ref[slab(s), cols]
                s1_ref[slab(t), cols] = p_ref[1, slab(t), cols] * acc

    def pool_branch():
        def zc(j, cols):
            if j < POOL_MAX - 1:
                return stp_ref[j, :, cols]
            return p_ref[0, slab(j - (POOL_MAX - 1)), cols]

        for j in range(POOL_MAX - 1):
            pso_ref[j] = zc(j + DEC_SEQ, slice(0, E))
        for gi, w in enumerate(POOL_WINDOWS):
            cols = slice(gi * C_GC, (gi + 1) * C_GC)
            for t in range(DEC_SEQ):
                c = zc(POOL_MAX - 1 + t, cols)
                win = c
                for j in range(1, w):
                    win = win + zc(POOL_MAX - 1 + t - j, cols)
                s1_ref[slab(t), cols] = win / float(min(w, PAST_LEN + 1)) - c
        for gi in range(len(POOL_WINDOWS)):
            cols = slice(gi * C_GC, (gi + 1) * C_GC)
            y = jnp.dot(s1_ref[:, cols].astype(bf16), pw_ref[gi], preferred_element_type=f32)
            s1_ref[:, cols] = y * ve_ref[5:6, cols]

    _run_tile_layer((w_in_hbm, w_out_hbm, w_br_a_hbm, w_br_b_hbm, w_br_c_hbm), x_ref, vd_ref, out_ref,
                    lhs_ref, p_ref, s1_ref, su_ref, m_ref, wbuf_ref, w2buf_ref, wsem, w2sem,
                    tm, {1: conv_branch, 4: gmlp_branch, 6: pool_branch})


def _weight_specs():
    return [pl.BlockSpec(memory_space=pl.ANY)] * 5 + [
        pl.BlockSpec((None, 2, D_MODEL), lambda i, l: (l, 0, 0)),
        pl.BlockSpec((None, 6, E), lambda i, l: (l, 0, 0)),
        pl.BlockSpec((None, CONV_W, E), lambda i, l: (l, 0, 0)),
    ]


def _work_buffers(tm):
    return [
        pltpu.VMEM((tm, D_MODEL), bf16),
        pltpu.VMEM((2, tm, COL), f32),
        pltpu.VMEM((tm, E), f32),
        pltpu.VMEM((N_BRANCH, tm, E), bf16),
        pltpu.VMEM((tm, D_MODEL), f32),
        pltpu.VMEM((N_SLOTS, D_MODEL, COL), bf16),
        pltpu.VMEM((N_W2_SLOTS, E, COL), bf16),
        pltpu.SemaphoreType.DMA((N_SLOTS,)),
        pltpu.SemaphoreType.DMA((N_W2_SLOTS,)),
    ]


def _compiler_params():
    return pltpu.CompilerParams(dimension_semantics=("arbitrary", "arbitrary"),
                                vmem_limit_bytes=VMEM_LIMIT_BYTES)


def _prompt_group(depth, x2d, n_seq, seq_len, weights, vd, ve, cw, ws, bs_full, pw):
    tm = TM_PROMPT
    assert seq_len % tm == 0 and tm % CHUNK == 0
    tiles_per_seq = seq_len // tm
    n_rows = n_seq * seq_len
    grid = (n_rows // tm, depth)
    in_specs = [pl.BlockSpec((tm, D_MODEL), lambda i, l: (i, 0))] + _weight_specs() + [
        pl.BlockSpec((None, B_GROUPS, CHUNK, CHUNK), lambda i, l: (l, 0, 0, 0)),
        pl.BlockSpec((None, CHUNK, E), lambda i, l: (l, 0, 0)),
        pl.BlockSpec((None, len(POOL_WINDOWS), C_GC, C_GC), lambda i, l: (l, 0, 0, 0)),
    ]
    out_specs = [
        pl.BlockSpec((tm, D_MODEL), lambda i, l: (i, 0)),
        pl.BlockSpec((None, None, CONV_W - 1, E), lambda i, l: (l, i, 0, 0)),
        pl.BlockSpec((None, None, POOL_MAX - 1, E), lambda i, l: (l, i, 0, 0)),
    ]
    out_shape = [
        jax.ShapeDtypeStruct((n_rows, D_MODEL), f32),
        jax.ShapeDtypeStruct((depth, n_rows // tm, CONV_W - 1, E), f32),
        jax.ShapeDtypeStruct((depth, n_rows // tm, POOL_MAX - 1, E), f32),
    ]
    scratch = _work_buffers(tm) + [
        pltpu.VMEM((CONV_HALO + tm, E), f32),
        pltpu.VMEM((POOL_HALO + tm, E), f32),
        pltpu.VMEM((depth, CONV_HALO, E), f32),
        pltpu.VMEM((depth, POOL_HALO, E), f32),
    ]
    x_out, conv_tiles, pool_tiles = pl.pallas_call(
        functools.partial(_prompt_kernel, tiles_per_seq=tiles_per_seq),
        grid=grid, in_specs=in_specs, out_specs=out_specs, out_shape=out_shape,
        scratch_shapes=scratch, compiler_params=_compiler_params(),
        name="prompt_group",
    )(x2d, *weights, vd, ve, cw, ws, bs_full, pw)
    last = slice(tiles_per_seq - 1, None, tiles_per_seq)
    return x_out, conv_tiles[:, last], pool_tiles[:, last]


def _sample_group(depth, x_sample, weights, vd, ve, cw, mix, b8, pw, state_conv, state_pool):
    tm = TM_SAMPLE
    n_dec = x_sample.shape[0]
    sq = tm // DEC_SEQ
    assert n_dec % sq == 0
    n_tiles = n_dec // sq
    x2d = jnp.transpose(x_sample.reshape(n_tiles, sq, DEC_SEQ, D_MODEL), (0, 2, 1, 3)).reshape(n_dec * DEC_SEQ, D_MODEL)
    stc = jnp.transpose(state_conv, (0, 2, 1, 3))
    stp = jnp.transpose(state_pool, (0, 2, 1, 3))
    grid = (n_tiles, depth)
    in_specs = [pl.BlockSpec((tm, D_MODEL), lambda i, l: (i, 0))] + _weight_specs() + [
        pl.BlockSpec((None, DEC_SEQ, DEC_SEQ, E), lambda i, l: (l, 0, 0, 0)),
        pl.BlockSpec((None, DEC_SEQ, E), lambda i, l: (l, 0, 0)),
        pl.BlockSpec((None, len(POOL_WINDOWS), C_GC, C_GC), lambda i, l: (l, 0, 0, 0)),
        pl.BlockSpec((None, CONV_W - 1, sq, E), lambda i, l: (l, 0, i, 0)),
        pl.BlockSpec((None, POOL_MAX - 1, sq, E), lambda i, l: (l, 0, i, 0)),
    ]
    out_specs = [
        pl.BlockSpec((tm, D_MODEL), lambda i, l: (i, 0)),
        pl.BlockSpec((None, CONV_W - 1, sq, E), lambda i, l: (l, 0, i, 0)),
        pl.BlockSpec((None, POOL_MAX - 1, sq, E), lambda i, l: (l, 0, i, 0)),
        pl.BlockSpec((None, DEC_SEQ, sq, E), lambda i, l: (l, 0, i, 0)),
    ]
    out_shape = [
        jax.ShapeDtypeStruct((n_dec * DEC_SEQ, D_MODEL), f32),
        jax.ShapeDtypeStruct((depth, CONV_W - 1, n_dec, E), f32),
        jax.ShapeDtypeStruct((depth, POOL_MAX - 1, n_dec, E), f32),
        jax.ShapeDtypeStruct((depth, DEC_SEQ, n_dec, E), f32),
    ]
    x_out, conv_s, pool_s, v_s = pl.pallas_call(
        _sample_kernel,
        grid=grid, in_specs=in_specs, out_specs=out_specs, out_shape=out_shape,
        scratch_shapes=_work_buffers(tm), compiler_params=_compiler_params(),
        name="sample_group",
    )(x2d, *weights, vd, ve, cw, mix, b8, pw, stc, stp)
    x_out = jnp.transpose(x_out.reshape(n_tiles, DEC_SEQ, sq, D_MODEL), (0, 2, 1, 3)).reshape(n_dec, DEC_SEQ, D_MODEL)
    back = lambda a: jnp.transpose(a, (0, 2, 1, 3))
    return x_out, back(conv_s), back(pool_s), back(v_s)


def _sample_mix_tables(gmlp_ws, gmlp_bs):
    mix = jnp.repeat(jnp.transpose(gmlp_ws[:, :, :DEC_SEQ, :DEC_SEQ], (0, 2, 3, 1)), B_GC, axis=-1)
    b8 = jnp.repeat(jnp.transpose(gmlp_bs[:, :, :DEC_SEQ], (0, 2, 1)), B_GC, axis=-1)
    return mix, b8


def kernel(x_prompt, x_sample, state_conv, state_pool, g_pre, w_in, conv_w, conv_b, conv_ln_g, conv_ln_b,
           w_br_a, gmlp_ln_g, gmlp_ln_b, gmlp_ws, gmlp_bs, w_br_b, pool_w, pool_scale, w_br_c, w_out, g_post):
    depth = w_in.shape[0]
    n_seq, seq_len, _ = x_prompt.shape
    n_dec, dec_seq, _ = x_sample.shape
    assert dec_seq == DEC_SEQ and x_prompt.shape[2] == D_MODEL

    weights = tuple(w.astype(bf16) for w in (w_in, w_out, w_br_a, w_br_b, w_br_c))
    vd = jnp.stack([g_pre, g_post], axis=1)
    ve = jnp.stack([conv_b, conv_ln_g, conv_ln_b, gmlp_ln_g, gmlp_ln_b, pool_scale], axis=1)
    bs_full = jnp.repeat(jnp.transpose(gmlp_bs, (0, 2, 1)), B_GC, axis=-1)
    pw = pool_w.astype(bf16)
    mix, b8 = _sample_mix_tables(gmlp_ws, gmlp_bs)

    xp, conv_p, pool_p = _prompt_group(depth, x_prompt.reshape(n_seq * seq_len, D_MODEL), n_seq, seq_len,
                                       weights, vd, ve, conv_w, gmlp_ws, bs_full, pw)
    xs, conv_s, pool_s, v_s = _sample_group(depth, x_sample, weights, vd, ve, conv_w, mix, b8, pw,
                                            state_conv, state_pool)
    return (xp.reshape(n_seq, seq_len, D_MODEL), xs, conv_p, pool_p, conv_s, pool_s, v_s)
```

```python
import functools

import jax
import jax.numpy as jnp
import numpy as np
from jax import lax
from jax.experimental import pallas as pl
from jax.experimental.pallas import tpu as pltpu

D_MODEL = 2048
E = 1024
N_BRANCH = 3
CONV_W = 31
CONV_HALO = 32
SUBLANES = 8
LANES = 128
CHUNK = 128
B_GROUPS = 8
B_GC = E // B_GROUPS
POOL_WINDOWS = (2, 4, 8, 16)
C_GC = E // len(POOL_WINDOWS)
POOL_MAX = 16
POOL_HALO = 32
DEC_SEQ = 8
PAST_LEN = 16384
EPS = 1e-6
COL = 1024
N_IN_BLOCKS = (8 * E + N_BRANCH * D_MODEL) // COL
N_BLOCKS = N_IN_BLOCKS + D_MODEL // COL
FIRST_MERGE_BLOCK = 8
N_MERGE = 2 * N_BRANCH
LOOKAHEAD = 2
N_SLOTS = LOOKAHEAD + 1
N_W2_SLOTS = 2

TM_PROMPT = 512
TM_SAMPLE = 256
VMEM_LIMIT_BYTES = 60 * 1024 * 1024

f32 = jnp.float32
bf16 = jnp.bfloat16


def _for_rows(n_rows, rc, fn):
    def body(c, carry):
        fn(pl.multiple_of(c * rc, rc))
        return carry
    lax.fori_loop(0, n_rows // rc, body, 0)


def _sigmoid(x):
    return 0.5 * jnp.tanh(0.5 * x) + 0.5


def _silu(x):
    h = 0.5 * x
    return h * (jnp.tanh(h) + 1.0)


def _layernorm_rows(y, g, b):
    mu = jnp.mean(y, axis=-1, keepdims=True)
    yc = y - mu
    var = jnp.mean(yc * yc, axis=-1, keepdims=True)
    return yc * lax.rsqrt(var + EPS) * g + b


def _start_layer(l, x_ref, vd_ref, out_ref, lhs_ref, tm):
    @pl.when(l == 0)
    def _():
        def load(r0):
            out_ref[pl.ds(r0, 128), :] = x_ref[pl.ds(r0, 128), :]
        _for_rows(tm, 128, load)

    def norm(r0):
        r = pl.ds(r0, 128)
        x = out_ref[r, :]
        ms = jnp.mean(x * x, axis=-1, keepdims=True)
        lhs_ref[r, :] = (x * lax.rsqrt(ms + EPS) * vd_ref[0:1, :]).astype(bf16)
    _for_rows(tm, 128, norm)


def _widen(p_ref, s1_ref, su_ref, tm):
    def go(r0):
        r = pl.ds(r0, 64)
        su_ref[r, :] = (s1_ref[r, :] * _silu(p_ref[r, :])).astype(bf16)
    _for_rows(tm, 64, go)


def _merge(p_ref, su_ref, w2_ref, m_ref, half, first):
    cols = slice(half * COL, (half + 1) * COL)
    y = jnp.dot(su_ref[...], w2_ref[...], preferred_element_type=f32)
    g = _sigmoid(p_ref[...]) * y
    if first:
        m_ref[:, cols] = g
    else:
        m_ref[:, cols] += g


def _m_to_lhs(m_ref, lhs_ref, tm):
    def go(r0):
        r = pl.ds(r0, 64)
        lhs_ref[r, :] = m_ref[r, :].astype(bf16)
    _for_rows(tm, 64, go)


def _post_norm_residual(y0_ref, y1_ref, vd_ref, x_ref, tm):
    def go(r0):
        r = pl.ds(r0, 128)
        y0 = y0_ref[r, :]
        y1 = y1_ref[r, :]
        ss = jnp.sum(y0 * y0, axis=-1, keepdims=True) + jnp.sum(y1 * y1, axis=-1, keepdims=True)
        inv = lax.rsqrt(ss / D_MODEL + EPS)
        x_ref[r, 0:COL] = x_ref[r, 0:COL] + y0 * inv * vd_ref[1:2, 0:COL]
        x_ref[r, COL:D_MODEL] = x_ref[r, COL:D_MODEL] + y1 * inv * vd_ref[1:2, COL:D_MODEL]
    _for_rows(tm, 128, go)


def _ln_rows(src_ref, dst_ref, g, b, tm, swish):
    rc = min(tm, 256)
    def go(r0):
        r = pl.ds(r0, rc)
        y = _layernorm_rows(src_ref[r, :], g, b)
        dst_ref[r, :] = _silu(y) if swish else y
    _for_rows(tm, rc, go)


def _run_tile_layer(weights, x_ref, vd_ref, out_ref,
                    lhs_ref, p_ref, s1_ref, su_ref, m_ref, wbuf_ref, w2buf_ref, wsem, w2sem,
                    tm, branch_steps, proj_dst):
    w_in_hbm, w_out_hbm = weights[0], weights[1]
    w_br_hbm = weights[2:]
    i, l = pl.program_id(0), pl.program_id(1)
    n_tiles, depth = pl.num_programs(0), pl.num_programs(1)
    is_first = jnp.logical_and(i == 0, l == 0)
    is_last = jnp.logical_and(i == n_tiles - 1, l == depth - 1)
    next_l = jnp.where(l + 1 < depth, l + 1, 0)

    ring_base = lax.rem((i * depth + l) * N_BLOCKS, N_SLOTS)

    def slot_of(k):
        return lax.rem(ring_base + k, N_SLOTS)

    def w_copy(layer, k):
        slot = slot_of(k)
        kk = k % N_BLOCKS
        src, kk = (w_in_hbm, kk) if kk < N_IN_BLOCKS else (w_out_hbm, kk - N_IN_BLOCKS)
        return pltpu.make_async_copy(src.at[layer, :, pl.ds(kk * COL, COL)], wbuf_ref.at[slot], wsem.at[slot])

    def w2_copy(j):
        slot = j % N_W2_SLOTS
        return pltpu.make_async_copy(w_br_hbm[j // 2].at[l, :, pl.ds((j % 2) * COL, COL)],
                                     w2buf_ref.at[slot], w2sem.at[slot])

    @pl.when(is_first)
    def _():
        for k in range(LOOKAHEAD):
            w_copy(l, k).start()

    _start_layer(l, x_ref, vd_ref, out_ref, lhs_ref, tm)

    for k in range(N_BLOCKS):
        w_copy(l, k).wait()
        if k + LOOKAHEAD < N_BLOCKS:
            w_copy(l, k + LOOKAHEAD).start()
        else:
            @pl.when(jnp.logical_not(is_last))
            def _(k=k):
                w_copy(next_l, k + LOOKAHEAD).start()
        if k == FIRST_MERGE_BLOCK - 1:
            w2_copy(0).start()

        proj = jnp.dot(lhs_ref[...], wbuf_ref[slot_of(k)], preferred_element_type=f32)
        if k in proj_dst:
            proj_dst[k][...] = proj
        else:
            p_ref[k % 2] = proj

        if k in branch_steps:
            branch_steps[k]()
        if k in (2, 5, 7):
            _widen(p_ref.at[k % 2], s1_ref, su_ref.at[(2, 5, 7).index(k)], tm)
        if FIRST_MERGE_BLOCK <= k < N_IN_BLOCKS:
            j = k - FIRST_MERGE_BLOCK
            w2_copy(j).wait()
            if j + 1 < N_MERGE:
                w2_copy(j + 1).start()
            _merge(p_ref.at[k % 2], su_ref.at[j // 2], w2buf_ref.at[j % N_W2_SLOTS], m_ref, j % 2, first=j < 2)
        if k == N_IN_BLOCKS - 1:
            _m_to_lhs(m_ref, lhs_ref, tm)
        if k == N_BLOCKS - 1:
            _post_norm_residual(p_ref.at[0], p_ref.at[1], vd_ref, out_ref, tm)


def _conv_rows_prompt(zbuf_ref, cw_ref, ve_ref, dst_ref, r0, rc):
    for cb in range(E // LANES):
        cols = slice(cb * LANES, (cb + 1) * LANES)
        blk = zbuf_ref[pl.ds(r0, rc + CONV_HALO), cols]
        lead = CONV_HALO - (CONV_W - 1)
        y = ve_ref[0:1, cols] + cw_ref[CONV_W - 1:CONV_W, cols] * blk[CONV_HALO:CONV_HALO + rc, :]
        for b in range(SUBLANES):
            acc = None
            for a in range(CONV_HALO // SUBLANES):
                k = SUBLANES * a + b - lead
                if k < 0:
                    continue
                term = cw_ref[k:k + 1, cols] * blk[SUBLANES * a:SUBLANES * a + rc + SUBLANES, :]
                acc = term if acc is None else acc + term
            y = y + acc[b:b + rc, :]
        dst_ref[pl.ds(r0, rc), cols] = y


def _trailing_sum(blk, w):
    rows = blk.shape[0]
    win, lo, span = blk, 0, 1
    while span < w:
        n = rows - (lo + SUBLANES)
        win = win[SUBLANES:SUBLANES + n, :] + win[SUBLANES - span:SUBLANES - span + n, :]
        lo += SUBLANES
        span *= 2
    return win[POOL_HALO - lo:, :]


def _prompt_kernel(x_ref, w_in_hbm, w_out_hbm, w_br_a_hbm, w_br_b_hbm, w_br_c_hbm,
                   vd_ref, ve_ref, cw_ref, ws_ref, bs_ref, pw_ref,
                   out_ref, cst_ref, pst_ref,
                   lhs_ref, p_ref, s1_ref, su_ref, m_ref, wbuf_ref, w2buf_ref, wsem, w2sem,
                   zbuf_ref, cbuf_ref, zhalo_ref, chalo_ref,
                   *, tiles_per_seq):
    tm = TM_PROMPT
    l = pl.program_id(1)
    q = pl.program_id(0) % tiles_per_seq

    def conv_branch():
        @pl.when(q == 0)
        def _():
            zbuf_ref[0:CONV_HALO, :] = jnp.zeros((CONV_HALO, E), f32)

        @pl.when(q > 0)
        def _():
            zbuf_ref[0:CONV_HALO, :] = zhalo_ref[l]

        def glu(r0):
            r = pl.ds(r0, 64)
            zbuf_ref[pl.ds(CONV_HALO + r0, 64), :] = p_ref[0, r, :] * _sigmoid(p_ref[1, r, :])
        _for_rows(tm, 64, glu)

        _for_rows(tm, CHUNK, lambda r0: _conv_rows_prompt(zbuf_ref, cw_ref, ve_ref, s1_ref, r0, CHUNK))
        _ln_rows(s1_ref, s1_ref, ve_ref[1:2, :], ve_ref[2:3, :], tm, swish=True)

        cst_ref[...] = zbuf_ref[CONV_HALO + tm - (CONV_W - 1):CONV_HALO + tm, :]
        zhalo_ref[l] = zbuf_ref[tm:tm + CONV_HALO, :]

    def gmlp_branch():
        _ln_rows(p_ref.at[0], s1_ref, ve_ref[3:4, :], ve_ref[4:5, :], tm, swish=False)
        tril = (lax.broadcasted_iota(jnp.int32, (CHUNK, CHUNK), 0)
                >= lax.broadcasted_iota(jnp.int32, (CHUNK, CHUNK), 1))
        for grp in range(B_GROUPS):
            cols = slice(grp * B_GC, (grp + 1) * B_GC)
            w_g = jnp.where(tril, ws_ref[grp], 0.0).astype(bf16)
            for c in range(tm // CHUNK):
                rows = slice(c * CHUNK, (c + 1) * CHUNK)
                v = s1_ref[rows, cols].astype(bf16)
                mixed = jnp.dot(w_g, v, preferred_element_type=f32) + bs_ref[:, cols]
                s1_ref[rows, cols] = p_ref[1, rows, cols] * mixed

    def pool_branch():
        @pl.when(q == 0)
        def _():
            cbuf_ref[0:POOL_HALO, :] = jnp.zeros((POOL_HALO, E), f32)

        @pl.when(q > 0)
        def _():
            cbuf_ref[0:POOL_HALO, :] = chalo_ref[l]

        def pool(r0, full_windows):
            pos = q * tm + r0 + lax.broadcasted_iota(jnp.int32, (CHUNK, C_GC), 0)
            for gi, w in enumerate(POOL_WINDOWS):
                cols = slice(gi * C_GC, (gi + 1) * C_GC)
                blk = cbuf_ref[pl.ds(r0, POOL_HALO + CHUNK), cols]
                c = blk[POOL_HALO:, :]
                if full_windows:
                    mean = _trailing_sum(blk, w) * (1.0 / w)
                else:
                    mean = _trailing_sum(blk, w) / jnp.minimum(w, pos + 1).astype(f32)
                y = jnp.dot((mean - c).astype(bf16), pw_ref[gi], preferred_element_type=f32)
                s1_ref[pl.ds(r0, CHUNK), cols] = y * ve_ref[5:6, cols]

        def pool_chunk(r0):
            starts_sequence = jnp.logical_and(q == 0, r0 == 0)
            pl.when(starts_sequence)(lambda: pool(r0, full_windows=False))
            pl.when(jnp.logical_not(starts_sequence))(lambda: pool(r0, full_windows=True))
        assert CHUNK >= POOL_MAX
        _for_rows(tm, CHUNK, pool_chunk)

        pst_ref[...] = cbuf_ref[POOL_HALO + tm - (POOL_MAX - 1):POOL_HALO + tm, :]
        chalo_ref[l] = cbuf_ref[tm:tm + POOL_HALO, :]

    _run_tile_layer((w_in_hbm, w_out_hbm, w_br_a_hbm, w_br_b_hbm, w_br_c_hbm), x_ref, vd_ref, out_ref,
                    lhs_ref, p_ref, s1_ref, su_ref, m_ref, wbuf_ref, w2buf_ref, wsem, w2sem,
                    tm, {1: conv_branch, 4: gmlp_branch, 6: pool_branch},
                    proj_dst={6: cbuf_ref.at[pl.ds(POOL_HALO, tm), :]})


def _sample_kernel(x_ref, w_in_hbm, w_out_hbm, w_br_a_hbm, w_br_b_hbm, w_br_c_hbm,
                   vd_ref, ve_ref, cw_ref, mix_ref, b8_ref, pw_ref, stc_ref, stp_ref,
                   out_ref, cso_ref, pso_ref, v_ref,
                   lhs_ref, p_ref, s1_ref, su_ref, m_ref, wbuf_ref, w2buf_ref, wsem, w2sem):
    tm = TM_SAMPLE
    sq = tm // DEC_SEQ
    cbw = 256

    def slab(t):
        return slice(t * sq, (t + 1) * sq)

    def conv_branch():
        def glu(r0):
            r = pl.ds(r0, 64)
            m_ref[r, 0:E] = p_ref[0, r, :] * _sigmoid(p_ref[1, r, :])
        _for_rows(tm, 64, glu)

        def zp(j, cols):
            if j < CONV_W - 1:
                return stc_ref[j, :, cols]
            return m_ref[slab(j - (CONV_W - 1)), cols]

        for j in range(CONV_W - 1):
            cso_ref[j] = zp(j + DEC_SEQ, slice(0, E))
        for cb in range(E // cbw):
            cols = slice(cb * cbw, (cb + 1) * cbw)
            for t in range(DEC_SEQ):
                acc = jnp.broadcast_to(ve_ref[0:1, cols], (sq, cbw))
                for k in range(CONV_W):
                    acc = acc + cw_ref[k:k + 1, cols] * zp(t + k, cols)
                s1_ref[slab(t), cols] = acc
        _ln_rows(s1_ref, s1_ref, ve_ref[1:2, :], ve_ref[2:3, :], tm, swish=True)

    def gmlp_branch():
        _ln_rows(p_ref.at[0], m_ref.at[:, 0:E], ve_ref[3:4, :], ve_ref[4:5, :], tm, swish=False)
        for t in range(DEC_SEQ):
            v_ref[t] = m_ref[slab(t), 0:E]
        for cb in range(E // cbw):
            cols = slice(cb * cbw, (cb + 1) * cbw)
            for t in range(DEC_SEQ):
                acc = jnp.broadcast_to(b8_ref[t:t + 1, cols], (sq, cbw))
                for s in range(t + 1):
                    acc = acc + mix_ref[t, s:s + 1, cols] * m_ref[slab(s), cols]
                s1_ref[slab(t), cols] = p_ref[1, slab(t), cols] * acc

    def pool_branch():
        def zc(j, cols):
            if j < POOL_MAX - 1:
                return stp_ref[j, :, cols]
            return p_ref[0, slab(j - (POOL_MAX - 1)), cols]

        for j in range(POOL_MAX - 1):
            pso_ref[j] = zc(j + DEC_SEQ, slice(0, E))
        for gi, w in enumerate(POOL_WINDOWS):
            cols = slice(gi * C_GC, (gi + 1) * C_GC)
            for t in range(DEC_SEQ):
                c = zc(POOL_MAX - 1 + t, cols)
                win = c
                for j in range(1, w):
                    win = win + zc(POOL_MAX - 1 + t - j, cols)
                s1_ref[slab(t), cols] = win / float(min(w, PAST_LEN + 1)) - c
        for gi in range(len(POOL_WINDOWS)):
            cols = slice(gi * C_GC, (gi + 1) * C_GC)
            y = jnp.dot(s1_ref[:, cols].astype(bf16), pw_ref[gi], preferred_element_type=f32)
            s1_ref[:, cols] = y * ve_ref[5:6, cols]

    _run_tile_layer((w_in_hbm, w_out_hbm, w_br_a_hbm, w_br_b_hbm, w_br_c_hbm), x_ref, vd_ref, out_ref,
                    lhs_ref, p_ref, s1_ref, su_ref, m_ref, wbuf_ref, w2buf_ref, wsem, w2sem,
                    tm, {1: conv_branch, 4: gmlp_branch, 6: pool_branch}, proj_dst={})


def _weight_specs():
    return [pl.BlockSpec(memory_space=pl.ANY)] * 5 + [
        pl.BlockSpec((None, 2, D_MODEL), lambda i, l: (l, 0, 0)),
        pl.BlockSpec((None, 6, E), lambda i, l: (l, 0, 0)),
        pl.BlockSpec((None, CONV_W, E), lambda i, l: (l, 0, 0)),
    ]


def _work_buffers(tm):
    return [
        pltpu.VMEM((tm, D_MODEL), bf16),
        pltpu.VMEM((2, tm, COL), f32),
        pltpu.VMEM((tm, E), f32),
        pltpu.VMEM((N_BRANCH, tm, E), bf16),
        pltpu.VMEM((tm, D_MODEL), f32),
        pltpu.VMEM((N_SLOTS, D_MODEL, COL), bf16),
        pltpu.VMEM((N_W2_SLOTS, E, COL), bf16),
        pltpu.SemaphoreType.DMA((N_SLOTS,)),
        pltpu.SemaphoreType.DMA((N_W2_SLOTS,)),
    ]


def _compiler_params():
    return pltpu.CompilerParams(dimension_semantics=("arbitrary", "arbitrary"),
                                vmem_limit_bytes=VMEM_LIMIT_BYTES)


def _prompt_group(depth, x2d, n_seq, seq_len, weights, vd, ve, cw, ws, bs_full, pw):
    tm = TM_PROMPT
    assert seq_len % tm == 0 and tm % CHUNK == 0
    tiles_per_seq = seq_len // tm
    n_rows = n_seq * seq_len
    grid = (n_rows // tm, depth)
    in_specs = [pl.BlockSpec((tm, D_MODEL), lambda i, l: (i, 0))] + _weight_specs() + [
        pl.BlockSpec((None, B_GROUPS, CHUNK, CHUNK), lambda i, l: (l, 0, 0, 0)),
        pl.BlockSpec((None, CHUNK, E), lambda i, l: (l, 0, 0)),
        pl.BlockSpec((None, len(POOL_WINDOWS), C_GC, C_GC), lambda i, l: (l, 0, 0, 0)),
    ]
    out_specs = [
        pl.BlockSpec((tm, D_MODEL), lambda i, l: (i, 0)),
        pl.BlockSpec((None, None, CONV_W - 1, E), lambda i, l: (l, i, 0, 0)),
        pl.BlockSpec((None, None, POOL_MAX - 1, E), lambda i, l: (l, i, 0, 0)),
    ]
    out_shape = [
        jax.ShapeDtypeStruct((n_rows, D_MODEL), f32),
        jax.ShapeDtypeStruct((depth, n_rows // tm, CONV_W - 1, E), f32),
        jax.ShapeDtypeStruct((depth, n_rows // tm, POOL_MAX - 1, E), f32),
    ]
    scratch = _work_buffers(tm) + [
        pltpu.VMEM((CONV_HALO + tm, E), f32),
        pltpu.VMEM((POOL_HALO + tm, E), f32),
        pltpu.VMEM((depth, CONV_HALO, E), f32),
        pltpu.VMEM((depth, POOL_HALO, E), f32),
    ]
    x_out, conv_tiles, pool_tiles = pl.pallas_call(
        functools.partial(_prompt_kernel, tiles_per_seq=tiles_per_seq),
        grid=grid, in_specs=in_specs, out_specs=out_specs, out_shape=out_shape,
        scratch_shapes=scratch, compiler_params=_compiler_params(),
        name="prompt_group",
    )(x2d, *weights, vd, ve, cw, ws, bs_full, pw)
    last = slice(tiles_per_seq - 1, None, tiles_per_seq)
    return x_out, conv_tiles[:, last], pool_tiles[:, last]


def _sample_group(depth, x_sample, weights, vd, ve, cw, mix, b8, pw, state_conv, state_pool):
    tm = TM_SAMPLE
    n_dec = x_sample.shape[0]
    sq = tm // DEC_SEQ
    assert n_dec % sq == 0
    n_tiles = n_dec // sq
    x2d = jnp.transpose(x_sample.reshape(n_tiles, sq, DEC_SEQ, D_MODEL), (0, 2, 1, 3)).reshape(n_dec * DEC_SEQ, D_MODEL)
    stc = jnp.transpose(state_conv, (0, 2, 1, 3))
    stp = jnp.transpose(state_pool, (0, 2, 1, 3))
    grid = (n_tiles, depth)
    in_specs = [pl.BlockSpec((tm, D_MODEL), lambda i, l: (i, 0))] + _weight_specs() + [
        pl.BlockSpec((None, DEC_SEQ, DEC_SEQ, E), lambda i, l: (l, 0, 0, 0)),
        pl.BlockSpec((None, DEC_SEQ, E), lambda i, l: (l, 0, 0)),
        pl.BlockSpec((None, len(POOL_WINDOWS), C_GC, C_GC), lambda i, l: (l, 0, 0, 0)),
        pl.BlockSpec((None, CONV_W - 1, sq, E), lambda i, l: (l, 0, i, 0)),
        pl.BlockSpec((None, POOL_MAX - 1, sq, E), lambda i, l: (l, 0, i, 0)),
    ]
    out_specs = [
        pl.BlockSpec((tm, D_MODEL), lambda i, l: (i, 0)),
        pl.BlockSpec((None, CONV_W - 1, sq, E), lambda i, l: (l, 0, i, 0)),
        pl.BlockSpec((None, POOL_MAX - 1, sq, E), lambda i, l: (l, 0, i, 0)),
        pl.BlockSpec((None, DEC_SEQ, sq, E), lambda i, l: (l, 0, i, 0)),
    ]
    out_shape = [
        jax.ShapeDtypeStruct((n_dec * DEC_SEQ, D_MODEL), f32),
        jax.ShapeDtypeStruct((depth, CONV_W - 1, n_dec, E), f32),
        jax.ShapeDtypeStruct((depth, POOL_MAX - 1, n_dec, E), f32),
        jax.ShapeDtypeStruct((depth, DEC_SEQ, n_dec, E), f32),
    ]
    x_out, conv_s, pool_s, v_s = pl.pallas_call(
        _sample_kernel,
        grid=grid, in_specs=in_specs, out_specs=out_specs, out_shape=out_shape,
        scratch_shapes=_work_buffers(tm), compiler_params=_compiler_params(),
        name="sample_group",
    )(x2d, *weights, vd, ve, cw, mix, b8, pw, stc, stp)
    x_out = jnp.transpose(x_out.reshape(n_tiles, DEC_SEQ, sq, D_MODEL), (0, 2, 1, 3)).reshape(n_dec, DEC_SEQ, D_MODEL)
    back = lambda a: jnp.transpose(a, (0, 2, 1, 3))
    return x_out, back(conv_s), back(pool_s), back(v_s)


def _sample_mix_tables(gmlp_ws, gmlp_bs):
    mix = jnp.repeat(jnp.transpose(gmlp_ws[:, :, :DEC_SEQ, :DEC_SEQ], (0, 2, 3, 1)), B_GC, axis=-1)
    b8 = jnp.repeat(jnp.transpose(gmlp_bs[:, :, :DEC_SEQ], (0, 2, 1)), B_GC, axis=-1)
    return mix, b8


def kernel(x_prompt, x_sample, state_conv, state_pool, g_pre, w_in, conv_w, conv_b, conv_ln_g, conv_ln_b,
           w_br_a, gmlp_ln_g, gmlp_ln_b, gmlp_ws, gmlp_bs, w_br_b, pool_w, pool_scale, w_br_c, w_out, g_post):
    depth = w_in.shape[0]
    n_seq, seq_len, _ = x_prompt.shape
    n_dec, dec_seq, _ = x_sample.shape
    assert dec_seq == DEC_SEQ and x_prompt.shape[2] == D_MODEL

    weights = tuple(w.astype(bf16) for w in (w_in, w_out, w_br_a, w_br_b, w_br_c))
    vd = jnp.stack([g_pre, g_post], axis=1)
    ve = jnp.stack([conv_b, conv_ln_g, conv_ln_b, gmlp_ln_g, gmlp_ln_b, pool_scale], axis=1)
    bs_full = jnp.repeat(jnp.transpose(gmlp_bs, (0, 2, 1)), B_GC, axis=-1)
    pw = pool_w.astype(bf16)
    mix, b8 = _sample_mix_tables(gmlp_ws, gmlp_bs)

    xp, conv_p, pool_p = _prompt_group(depth, x_prompt.reshape(n_seq * seq_len, D_MODEL), n_seq, seq_len,
                                       weights, vd, ve, conv_w, gmlp_ws, bs_full, pw)
    xs, conv_s, pool_s, v_s = _sample_group(depth, x_sample, weights, vd, ve, conv_w, mix, b8, pw,
                                            state_conv, state_pool)
    return (xp.reshape(n_seq, seq_len, D_MODEL), xs, conv_p, pool_p, conv_s, pool_s, v_s)
```

```python
import functools

import jax
import jax.numpy as jnp
import numpy as np
from jax import lax
from jax.experimental import pallas as pl
from jax.experimental.pallas import tpu as pltpu

D_MODEL = 2048
E = 1024
N_BRANCH = 3
CONV_W = 31
CONV_HALO = 32
SUBLANES = 8
LANES = 128
CHUNK = 128
B_GROUPS = 8
B_GC = E // B_GROUPS
POOL_WINDOWS = (2, 4, 8, 16)
C_GC = E // len(POOL_WINDOWS)
POOL_MAX = 16
POOL_HALO = 32
DEC_SEQ = 8
PAST_LEN = 16384
EPS = 1e-6
COL = 1024
N_IN_BLOCKS = (8 * E + N_BRANCH * D_MODEL) // COL
N_BLOCKS = N_IN_BLOCKS + D_MODEL // COL
FIRST_MERGE_BLOCK = 8
N_MERGE = 2 * N_BRANCH
LOOKAHEAD = 2
N_SLOTS = LOOKAHEAD + 1
N_W2_SLOTS = 2

TM_PROMPT = 512
TM_SAMPLE = 256
VMEM_LIMIT_BYTES = 60 * 1024 * 1024

f32 = jnp.float32
bf16 = jnp.bfloat16


ROWS_PER_TRIP = 64
ROWS_PER_NORM_TRIP = 512


def _for_rows(n_rows, rc, fn):
    if n_rows == rc:
        fn(0)
        return

    def body(c, carry):
        fn(pl.multiple_of(c * rc, rc))
        return carry
    lax.fori_loop(0, n_rows // rc, body, 0)


def _sigmoid(x):
    return 0.5 * jnp.tanh(0.5 * x) + 0.5


def _silu(x):
    h = 0.5 * x
    return h * (jnp.tanh(h) + 1.0)


def _layernorm_rows(y, g, b):
    mu = jnp.mean(y, axis=-1, keepdims=True)
    yc = y - mu
    var = jnp.mean(yc * yc, axis=-1, keepdims=True)
    return yc * lax.rsqrt(var + EPS) * g + b


def _start_layer(l, x_ref, vd_ref, out_ref, lhs_ref, tm):
    @pl.when(l == 0)
    def _():
        def load(r0):
            out_ref[pl.ds(r0, ROWS_PER_TRIP), :] = x_ref[pl.ds(r0, ROWS_PER_TRIP), :]
        _for_rows(tm, ROWS_PER_TRIP, load)

    rc = min(tm, ROWS_PER_NORM_TRIP)
    def norm(r0):
        r = pl.ds(r0, rc)
        x = out_ref[r, :]
        ms = jnp.mean(x * x, axis=-1, keepdims=True)
        lhs_ref[r, :] = (x * lax.rsqrt(ms + EPS) * vd_ref[0:1, :]).astype(bf16)
    _for_rows(tm, rc, norm)


def _widen(p_ref, s1_ref, su_ref, tm):
    def go(r0):
        r = pl.ds(r0, ROWS_PER_TRIP)
        su_ref[r, :] = (s1_ref[r, :] * _silu(p_ref[r, :])).astype(bf16)
    _for_rows(tm, ROWS_PER_TRIP, go)


def _merge(p_ref, su_ref, w2_ref, m_ref, half, first):
    cols = slice(half * COL, (half + 1) * COL)
    y = jnp.dot(su_ref[...], w2_ref[...], preferred_element_type=f32)
    g = _sigmoid(p_ref[...]) * y
    if first:
        m_ref[:, cols] = g
    else:
        m_ref[:, cols] += g


def _m_to_lhs(m_ref, lhs_ref, tm):
    def go(r0):
        r = pl.ds(r0, ROWS_PER_TRIP)
        lhs_ref[r, :] = m_ref[r, :].astype(bf16)
    _for_rows(tm, ROWS_PER_TRIP, go)


def _post_norm_residual(y0_ref, y1_ref, vd_ref, x_ref, tm):
    rc = min(tm, ROWS_PER_NORM_TRIP)
    def go(r0):
        r = pl.ds(r0, rc)
        y0 = y0_ref[r, :]
        y1 = y1_ref[r, :]
        ss = jnp.sum(y0 * y0, axis=-1, keepdims=True) + jnp.sum(y1 * y1, axis=-1, keepdims=True)
        inv = lax.rsqrt(ss / D_MODEL + EPS)
        x_ref[r, 0:COL] = x_ref[r, 0:COL] + y0 * inv * vd_ref[1:2, 0:COL]
        x_ref[r, COL:D_MODEL] = x_ref[r, COL:D_MODEL] + y1 * inv * vd_ref[1:2, COL:D_MODEL]
    _for_rows(tm, rc, go)


def _ln_rows(src_ref, dst_ref, g, b, tm, swish):
    rc = min(tm, ROWS_PER_NORM_TRIP)
    def go(r0):
        r = pl.ds(r0, rc)
        y = _layernorm_rows(src_ref[r, :], g, b)
        dst_ref[r, :] = _silu(y) if swish else y
    _for_rows(tm, rc, go)


def _run_tile_layer(weights, x_ref, vd_ref, out_ref,
                    lhs_ref, p_ref, s1_ref, su_ref, m_ref, wbuf_ref, w2buf_ref, wsem, w2sem,
                    tm, branch_steps, proj_dst):
    w_in_hbm, w_out_hbm = weights[0], weights[1]
    w_br_hbm = weights[2:]
    i, l = pl.program_id(0), pl.program_id(1)
    n_tiles, depth = pl.num_programs(0), pl.num_programs(1)
    is_first = jnp.logical_and(i == 0, l == 0)
    is_last = jnp.logical_and(i == n_tiles - 1, l == depth - 1)
    next_l = jnp.where(l + 1 < depth, l + 1, 0)

    ring_base = lax.rem((i * depth + l) * N_BLOCKS, N_SLOTS)

    def slot_of(k):
        return lax.rem(ring_base + k, N_SLOTS)

    def w_copy(layer, k):
        slot = slot_of(k)
        kk = k % N_BLOCKS
        src, kk = (w_in_hbm, kk) if kk < N_IN_BLOCKS else (w_out_hbm, kk - N_IN_BLOCKS)
        return pltpu.make_async_copy(src.at[layer, :, pl.ds(kk * COL, COL)], wbuf_ref.at[slot], wsem.at[slot])

    def w2_copy(j):
        slot = j % N_W2_SLOTS
        return pltpu.make_async_copy(w_br_hbm[j // 2].at[l, :, pl.ds((j % 2) * COL, COL)],
                                     w2buf_ref.at[slot], w2sem.at[slot])

    @pl.when(is_first)
    def _():
        for k in range(LOOKAHEAD):
            w_copy(l, k).start()

    _start_layer(l, x_ref, vd_ref, out_ref, lhs_ref, tm)

    for k in range(N_BLOCKS):
        w_copy(l, k).wait()
        if k + LOOKAHEAD < N_BLOCKS:
            w_copy(l, k + LOOKAHEAD).start()
        else:
            @pl.when(jnp.logical_not(is_last))
            def _(k=k):
                w_copy(next_l, k + LOOKAHEAD).start()
        if k == FIRST_MERGE_BLOCK - 1:
            w2_copy(0).start()

        proj = jnp.dot(lhs_ref[...], wbuf_ref[slot_of(k)], preferred_element_type=f32)
        if k in proj_dst:
            proj_dst[k][...] = proj
        else:
            p_ref[k % 2] = proj

        if k in branch_steps:
            branch_steps[k]()
        if k in (2, 5, 7):
            _widen(p_ref.at[k % 2], s1_ref, su_ref.at[(2, 5, 7).index(k)], tm)
        if FIRST_MERGE_BLOCK <= k < N_IN_BLOCKS:
            j = k - FIRST_MERGE_BLOCK
            w2_copy(j).wait()
            if j + 1 < N_MERGE:
                w2_copy(j + 1).start()
            _merge(p_ref.at[k % 2], su_ref.at[j // 2], w2buf_ref.at[j % N_W2_SLOTS], m_ref, j % 2, first=j < 2)
        if k == N_IN_BLOCKS - 1:
            _m_to_lhs(m_ref, lhs_ref, tm)
        if k == N_BLOCKS - 1:
            _post_norm_residual(p_ref.at[0], p_ref.at[1], vd_ref, out_ref, tm)


def _conv_rows_prompt(zbuf_ref, cw_ref, ve_ref, dst_ref, r0, rc):
    for cb in range(E // LANES):
        cols = slice(cb * LANES, (cb + 1) * LANES)
        blk = zbuf_ref[pl.ds(r0, rc + CONV_HALO), cols]
        lead = CONV_HALO - (CONV_W - 1)
        y = ve_ref[0:1, cols] + cw_ref[CONV_W - 1:CONV_W, cols] * blk[CONV_HALO:CONV_HALO + rc, :]
        for b in range(SUBLANES):
            acc = None
            for a in range(CONV_HALO // SUBLANES):
                k = SUBLANES * a + b - lead
                if k < 0:
                    continue
                term = cw_ref[k:k + 1, cols] * blk[SUBLANES * a:SUBLANES * a + rc + SUBLANES, :]
                acc = term if acc is None else acc + term
            y = y + acc[b:b + rc, :]
        dst_ref[pl.ds(r0, rc), cols] = y


def _trailing_sum(blk, w):
    rows = blk.shape[0]
    win, lo, span = blk, 0, 1
    while span < w:
        n = rows - (lo + SUBLANES)
        win = win[SUBLANES:SUBLANES + n, :] + win[SUBLANES - span:SUBLANES - span + n, :]
        lo += SUBLANES
        span *= 2
    return win[POOL_HALO - lo:, :]


def _prompt_kernel(x_ref, w_in_hbm, w_out_hbm, w_br_a_hbm, w_br_b_hbm, w_br_c_hbm,
                   vd_ref, ve_ref, cw_ref, ws_ref, bs_ref, pw_ref,
                   out_ref, cst_ref, pst_ref,
                   lhs_ref, p_ref, s1_ref, su_ref, m_ref, wbuf_ref, w2buf_ref, wsem, w2sem,
                   zbuf_ref, cbuf_ref, zhalo_ref, chalo_ref,
                   *, tiles_per_seq):
    tm = TM_PROMPT
    l = pl.program_id(1)
    q = pl.program_id(0) % tiles_per_seq

    def conv_branch():
        @pl.when(q == 0)
        def _():
            zbuf_ref[0:CONV_HALO, :] = jnp.zeros((CONV_HALO, E), f32)

        @pl.when(q > 0)
        def _():
            zbuf_ref[0:CONV_HALO, :] = zhalo_ref[l]

        def glu(r0):
            r = pl.ds(r0, ROWS_PER_TRIP)
            zbuf_ref[pl.ds(CONV_HALO + r0, ROWS_PER_TRIP), :] = p_ref[0, r, :] * _sigmoid(p_ref[1, r, :])
        _for_rows(tm, ROWS_PER_TRIP, glu)

        _for_rows(tm, CHUNK, lambda r0: _conv_rows_prompt(zbuf_ref, cw_ref, ve_ref, s1_ref, r0, CHUNK))
        _ln_rows(s1_ref, s1_ref, ve_ref[1:2, :], ve_ref[2:3, :], tm, swish=True)

        cst_ref[...] = zbuf_ref[CONV_HALO + tm - (CONV_W - 1):CONV_HALO + tm, :]
        zhalo_ref[l] = zbuf_ref[tm:tm + CONV_HALO, :]

    def gmlp_branch():
        _ln_rows(p_ref.at[0], s1_ref, ve_ref[3:4, :], ve_ref[4:5, :], tm, swish=False)
        tril = (lax.broadcasted_iota(jnp.int32, (CHUNK, CHUNK), 0)
                >= lax.broadcasted_iota(jnp.int32, (CHUNK, CHUNK), 1))
        for grp in range(B_GROUPS):
            cols = slice(grp * B_GC, (grp + 1) * B_GC)
            w_g = jnp.where(tril, ws_ref[grp], 0.0).astype(bf16)
            for c in range(tm // CHUNK):
                rows = slice(c * CHUNK, (c + 1) * CHUNK)
                v = s1_ref[rows, cols].astype(bf16)
                mixed = jnp.dot(w_g, v, preferred_element_type=f32) + bs_ref[:, cols]
                s1_ref[rows, cols] = p_ref[1, rows, cols] * mixed

    def pool_branch():
        @pl.when(q == 0)
        def _():
            cbuf_ref[0:POOL_HALO, :] = jnp.zeros((POOL_HALO, E), f32)

        @pl.when(q > 0)
        def _():
            cbuf_ref[0:POOL_HALO, :] = chalo_ref[l]

        def pool(r0, full_windows):
            pos = q * tm + r0 + lax.broadcasted_iota(jnp.int32, (CHUNK, C_GC), 0)
            for gi, w in enumerate(POOL_WINDOWS):
                cols = slice(gi * C_GC, (gi + 1) * C_GC)
                blk = cbuf_ref[pl.ds(r0, POOL_HALO + CHUNK), cols]
                c = blk[POOL_HALO:, :]
                if full_windows:
                    mean = _trailing_sum(blk, w) * (1.0 / w)
                else:
                    mean = _trailing_sum(blk, w) / jnp.minimum(w, pos + 1).astype(f32)
                y = jnp.dot((mean - c).astype(bf16), pw_ref[gi], preferred_element_type=f32)
                s1_ref[pl.ds(r0, CHUNK), cols] = y * ve_ref[5:6, cols]

        def pool_chunk(r0):
            starts_sequence = jnp.logical_and(q == 0, r0 == 0)
            pl.when(starts_sequence)(lambda: pool(r0, full_windows=False))
            pl.when(jnp.logical_not(starts_sequence))(lambda: pool(r0, full_windows=True))
        assert CHUNK >= POOL_MAX
        _for_rows(tm, CHUNK, pool_chunk)

        pst_ref[...] = cbuf_ref[POOL_HALO + tm - (POOL_MAX - 1):POOL_HALO + tm, :]
        chalo_ref[l] = cbuf_ref[tm:tm + POOL_HALO, :]

    _run_tile_layer((w_in_hbm, w_out_hbm, w_br_a_hbm, w_br_b_hbm, w_br_c_hbm), x_ref, vd_ref, out_ref,
                    lhs_ref, p_ref, s1_ref, su_ref, m_ref, wbuf_ref, w2buf_ref, wsem, w2sem,
                    tm, {1: conv_branch, 4: gmlp_branch, 6: pool_branch},
                    proj_dst={6: cbuf_ref.at[pl.ds(POOL_HALO, tm), :]})


def _sample_kernel(x_ref, w_in_hbm, w_out_hbm, w_br_a_hbm, w_br_b_hbm, w_br_c_hbm,
                   vd_ref, ve_ref, cw_ref, mix_ref, b8_ref, pw_ref, stc_ref, stp_ref,
                   out_ref, cso_ref, pso_ref, v_ref,
                   lhs_ref, p_ref, s1_ref, su_ref, m_ref, wbuf_ref, w2buf_ref, wsem, w2sem):
    tm = TM_SAMPLE
    sq = tm // DEC_SEQ
    cbw = 256

    def slab(t):
        return slice(t * sq, (t + 1) * sq)

    def conv_branch():
        def glu(r0):
            r = pl.ds(r0, ROWS_PER_TRIP)
            m_ref[r, 0:E] = p_ref[0, r, :] * _sigmoid(p_ref[1, r, :])
        _for_rows(tm, ROWS_PER_TRIP, glu)

        def zp(j, cols):
            if j < CONV_W - 1:
                return stc_ref[j, :, cols]
            return m_ref[slab(j - (CONV_W - 1)), cols]

        for j in range(CONV_W - 1):
            cso_ref[j] = zp(j + DEC_SEQ, slice(0, E))
        for cb in range(E // cbw):
            cols = slice(cb * cbw, (cb + 1) * cbw)
            for t in range(DEC_SEQ):
                acc = jnp.broadcast_to(ve_ref[0:1, cols], (sq, cbw))
                for k in range(CONV_W):
                    acc = acc + cw_ref[k:k + 1, cols] * zp(t + k, cols)
                s1_ref[slab(t), cols] = acc
        _ln_rows(s1_ref, s1_ref, ve_ref[1:2, :], ve_ref[2:3, :], tm, swish=True)

    def gmlp_branch():
        _ln_rows(p_ref.at[0], m_ref.at[:, 0:E], ve_ref[3:4, :], ve_ref[4:5, :], tm, swish=False)
        for t in range(DEC_SEQ):
            v_ref[t] = m_ref[slab(t), 0:E]
        for cb in range(E // cbw):
            cols = slice(cb * cbw, (cb + 1) * cbw)
            for t in range(DEC_SEQ):
                acc = jnp.broadcast_to(b8_ref[t:t + 1, cols], (sq, cbw))
                for s in range(t + 1):
                    acc = acc + mix_ref[t, s:s + 1, cols] * m_ref[slab(s), cols]
                s1_ref[slab(t), cols] = p_ref[1, slab(t), cols] * acc

    def pool_branch():
        def zc(j, cols):
            if j < POOL_MAX - 1:
                return stp_ref[j, :, cols]
            return p_ref[0, slab(j - (POOL_MAX - 1)), cols]

        for j in range(POOL_MAX - 1):
            pso_ref[j] = zc(j + DEC_SEQ, slice(0, E))
        for gi, w in enumerate(POOL_WINDOWS):
            cols = slice(gi * C_GC, (gi + 1) * C_GC)
            for t in range(DEC_SEQ):
                c = zc(POOL_MAX - 1 + t, cols)
                win = c
                for j in range(1, w):
                    win = win + zc(POOL_MAX - 1 + t - j, cols)
                s1_ref[slab(t), cols] = win / float(min(w, PAST_LEN + 1)) - c
        for gi in range(len(POOL_WINDOWS)):
            cols = slice(gi * C_GC, (gi + 1) * C_GC)
            y = jnp.dot(s1_ref[:, cols].astype(bf16), pw_ref[gi], preferred_element_type=f32)
            s1_ref[:, cols] = y * ve_ref[5:6, cols]

    _run_tile_layer((w_in_hbm, w_out_hbm, w_br_a_hbm, w_br_b_hbm, w_br_c_hbm), x_ref, vd_ref, out_ref,
                    lhs_ref, p_ref, s1_ref, su_ref, m_ref, wbuf_ref, w2buf_ref, wsem, w2sem,
                    tm, {1: conv_branch, 4: gmlp_branch, 6: pool_branch}, proj_dst={})


def _weight_specs():
    return [pl.BlockSpec(memory_space=pl.ANY)] * 5 + [
        pl.BlockSpec((None, 2, D_MODEL), lambda i, l: (l, 0, 0)),
        pl.BlockSpec((None, 6, E), lambda i, l: (l, 0, 0)),
        pl.BlockSpec((None, CONV_W, E), lambda i, l: (l, 0, 0)),
    ]


def _work_buffers(tm):
    return [
        pltpu.VMEM((tm, D_MODEL), bf16),
        pltpu.VMEM((2, tm, COL), f32),
        pltpu.VMEM((tm, E), f32),
        pltpu.VMEM((N_BRANCH, tm, E), bf16),
        pltpu.VMEM((tm, D_MODEL), f32),
        pltpu.VMEM((N_SLOTS, D_MODEL, COL), bf16),
        pltpu.VMEM((N_W2_SLOTS, E, COL), bf16),
        pltpu.SemaphoreType.DMA((N_SLOTS,)),
        pltpu.SemaphoreType.DMA((N_W2_SLOTS,)),
    ]


def _compiler_params():
    return pltpu.CompilerParams(dimension_semantics=("arbitrary", "arbitrary"),
                                vmem_limit_bytes=VMEM_LIMIT_BYTES)


def _prompt_group(depth, x2d, n_seq, seq_len, weights, vd, ve, cw, ws, bs_full, pw):
    tm = TM_PROMPT
    assert seq_len % tm == 0 and tm % CHUNK == 0
    tiles_per_seq = seq_len // tm
    n_rows = n_seq * seq_len
    grid = (n_rows // tm, depth)
    in_specs = [pl.BlockSpec((tm, D_MODEL), lambda i, l: (i, 0))] + _weight_specs() + [
        pl.BlockSpec((None, B_GROUPS, CHUNK, CHUNK), lambda i, l: (l, 0, 0, 0)),
        pl.BlockSpec((None, CHUNK, E), lambda i, l: (l, 0, 0)),
        pl.BlockSpec((None, len(POOL_WINDOWS), C_GC, C_GC), lambda i, l: (l, 0, 0, 0)),
    ]
    out_specs = [
        pl.BlockSpec((tm, D_MODEL), lambda i, l: (i, 0)),
        pl.BlockSpec((None, None, CONV_W - 1, E), lambda i, l: (l, i, 0, 0)),
        pl.BlockSpec((None, None, POOL_MAX - 1, E), lambda i, l: (l, i, 0, 0)),
    ]
    out_shape = [
        jax.ShapeDtypeStruct((n_rows, D_MODEL), f32),
        jax.ShapeDtypeStruct((depth, n_rows // tm, CONV_W - 1, E), f32),
        jax.ShapeDtypeStruct((depth, n_rows // tm, POOL_MAX - 1, E), f32),
    ]
    scratch = _work_buffers(tm) + [
        pltpu.VMEM((CONV_HALO + tm, E), f32),
        pltpu.VMEM((POOL_HALO + tm, E), f32),
        pltpu.VMEM((depth, CONV_HALO, E), f32),
        pltpu.VMEM((depth, POOL_HALO, E), f32),
    ]
    x_out, conv_tiles, pool_tiles = pl.pallas_call(
        functools.partial(_prompt_kernel, tiles_per_seq=tiles_per_seq),
        grid=grid, in_specs=in_specs, out_specs=out_specs, out_shape=out_shape,
        scratch_shapes=scratch, compiler_params=_compiler_params(),
        name="prompt_group",
    )(x2d, *weights, vd, ve, cw, ws, bs_full, pw)
    last = slice(tiles_per_seq - 1, None, tiles_per_seq)
    return x_out, conv_tiles[:, last], pool_tiles[:, last]


def _sample_group(depth, x_sample, weights, vd, ve, cw, mix, b8, pw, state_conv, state_pool):
    tm = TM_SAMPLE
    n_dec = x_sample.shape[0]
    sq = tm // DEC_SEQ
    assert n_dec % sq == 0
    n_tiles = n_dec // sq
    x2d = jnp.transpose(x_sample.reshape(n_tiles, sq, DEC_SEQ, D_MODEL), (0, 2, 1, 3)).reshape(n_dec * DEC_SEQ, D_MODEL)
    stc = jnp.transpose(state_conv, (0, 2, 1, 3))
    stp = jnp.transpose(state_pool, (0, 2, 1, 3))
    grid = (n_tiles, depth)
    in_specs = [pl.BlockSpec((tm, D_MODEL), lambda i, l: (i, 0))] + _weight_specs() + [
        pl.BlockSpec((None, DEC_SEQ, DEC_SEQ, E), lambda i, l: (l, 0, 0, 0)),
        pl.BlockSpec((None, DEC_SEQ, E), lambda i, l: (l, 0, 0)),
        pl.BlockSpec((None, len(POOL_WINDOWS), C_GC, C_GC), lambda i, l: (l, 0, 0, 0)),
        pl.BlockSpec((None, CONV_W - 1, sq, E), lambda i, l: (l, 0, i, 0)),
        pl.BlockSpec((None, POOL_MAX - 1, sq, E), lambda i, l: (l, 0, i, 0)),
    ]
    out_specs = [
        pl.BlockSpec((tm, D_MODEL), lambda i, l: (i, 0)),
        pl.BlockSpec((None, CONV_W - 1, sq, E), lambda i, l: (l, 0, i, 0)),
        pl.BlockSpec((None, POOL_MAX - 1, sq, E), lambda i, l: (l, 0, i, 0)),
        pl.BlockSpec((None, DEC_SEQ, sq, E), lambda i, l: (l, 0, i, 0)),
    ]
    out_shape = [
        jax.ShapeDtypeStruct((n_dec * DEC_SEQ, D_MODEL), f32),
        jax.ShapeDtypeStruct((depth, CONV_W - 1, n_dec, E), f32),
        jax.ShapeDtypeStruct((depth, POOL_MAX - 1, n_dec, E), f32),
        jax.ShapeDtypeStruct((depth, DEC_SEQ, n_dec, E), f32),
    ]
    x_out, conv_s, pool_s, v_s = pl.pallas_call(
        _sample_kernel,
        grid=grid, in_specs=in_specs, out_specs=out_specs, out_shape=out_shape,
        scratch_shapes=_work_buffers(tm), compiler_params=_compiler_params(),
        name="sample_group",
    )(x2d, *weights, vd, ve, cw, mix, b8, pw, stc, stp)
    x_out = jnp.transpose(x_out.reshape(n_tiles, DEC_SEQ, sq, D_MODEL), (0, 2, 1, 3)).reshape(n_dec, DEC_SEQ, D_MODEL)
    back = lambda a: jnp.transpose(a, (0, 2, 1, 3))
    return x_out, back(conv_s), back(pool_s), back(v_s)


def _sample_mix_tables(gmlp_ws, gmlp_bs):
    mix = jnp.repeat(jnp.transpose(gmlp_ws[:, :, :DEC_SEQ, :DEC_SEQ], (0, 2, 3, 1)), B_GC, axis=-1)
    b8 = jnp.repeat(jnp.transpose(gmlp_bs[:, :, :DEC_SEQ], (0, 2, 1)), B_GC, axis=-1)
    return mix, b8


def kernel(x_prompt, x_sample, state_conv, state_pool, g_pre, w_in, conv_w, conv_b, conv_ln_g, conv_ln_b,
           w_br_a, gmlp_ln_g, gmlp_ln_b, gmlp_ws, gmlp_bs, w_br_b, pool_w, pool_scale, w_br_c, w_out, g_post):
    depth = w_in.shape[0]
    n_seq, seq_len, _ = x_prompt.shape
    n_dec, dec_seq, _ = x_sample.shape
    assert dec_seq == DEC_SEQ and x_prompt.shape[2] == D_MODEL

    weights = tuple(w.astype(bf16) for w in (w_in, w_out, w_br_a, w_br_b, w_br_c))
    vd = jnp.stack([g_pre, g_post], axis=1)
    ve = jnp.stack([conv_b, conv_ln_g, conv_ln_b, gmlp_ln_g, gmlp_ln_b, pool_scale], axis=1)
    bs_full = jnp.repeat(jnp.transpose(gmlp_bs, (0, 2, 1)), B_GC, axis=-1)
    pw = pool_w.astype(bf16)
    mix, b8 = _sample_mix_tables(gmlp_ws, gmlp_bs)

    xp, conv_p, pool_p = _prompt_group(depth, x_prompt.reshape(n_seq * seq_len, D_MODEL), n_seq, seq_len,
                                       weights, vd, ve, conv_w, gmlp_ws, bs_full, pw)
    xs, conv_s, pool_s, v_s = _sample_group(depth, x_sample, weights, vd, ve, conv_w, mix, b8, pw,
                                            state_conv, state_pool)
    return (xp.reshape(n_seq, seq_len, D_MODEL), xs, conv_p, pool_p, conv_s, pool_s, v_s)
```

```python
import functools

import jax
import jax.numpy as jnp
import numpy as np
from jax import lax
from jax.experimental import pallas as pl
from jax.experimental.pallas import tpu as pltpu

D_MODEL = 2048
E = 1024
N_BRANCH = 3
CONV_W = 31
CONV_HALO = 32
SUBLANES = 8
LANES = 128
CHUNK = 128
B_GROUPS = 8
B_GC = E // B_GROUPS
POOL_WINDOWS = (2, 4, 8, 16)
C_GC = E // len(POOL_WINDOWS)
POOL_MAX = 16
POOL_HALO = 32
DEC_SEQ = 8
PAST_LEN = 16384
EPS = 1e-6
COL = 1024
N_IN_BLOCKS = (8 * E + N_BRANCH * D_MODEL) // COL
N_BLOCKS = N_IN_BLOCKS + D_MODEL // COL
FIRST_MERGE_BLOCK = 8
N_MERGE = 2 * N_BRANCH
LOOKAHEAD = 2
N_SLOTS = LOOKAHEAD + 1
N_W2_SLOTS = 2

TM_PROMPT = 512
TM_SAMPLE = 256
VMEM_LIMIT_BYTES = 60 * 1024 * 1024

f32 = jnp.float32
bf16 = jnp.bfloat16


ROWS_PER_TRIP = 64
ROWS_PER_NORM_TRIP = 512


def _for_rows(n_rows, rc, fn, rolled=False):
    if not rolled:
        for r0 in range(0, n_rows, rc):
            fn(r0)
        return

    def body(c, carry):
        fn(pl.multiple_of(c * rc, rc))
        return carry
    lax.fori_loop(0, n_rows // rc, body, 0)


def _sigmoid(x):
    return 0.5 * jnp.tanh(0.5 * x) + 0.5


def _silu(x):
    h = 0.5 * x
    return h * (jnp.tanh(h) + 1.0)


def _layernorm_rows(y, g, b):
    mu = jnp.mean(y, axis=-1, keepdims=True)
    yc = y - mu
    var = jnp.mean(yc * yc, axis=-1, keepdims=True)
    return yc * lax.rsqrt(var + EPS) * g + b


def _start_layer(l, x_ref, vd_ref, out_ref, lhs_ref, tm):
    @pl.when(l == 0)
    def _():
        def load(r0):
            out_ref[pl.ds(r0, ROWS_PER_TRIP), :] = x_ref[pl.ds(r0, ROWS_PER_TRIP), :]
        _for_rows(tm, ROWS_PER_TRIP, load)

    rc = min(tm, ROWS_PER_NORM_TRIP)
    def norm(r0):
        r = pl.ds(r0, rc)
        x = out_ref[r, :]
        ms = jnp.mean(x * x, axis=-1, keepdims=True)
        lhs_ref[r, :] = (x * lax.rsqrt(ms + EPS) * vd_ref[0:1, :]).astype(bf16)
    _for_rows(tm, rc, norm)


def _widen(p_ref, s1_ref, su_ref, tm):
    def go(r0):
        r = pl.ds(r0, ROWS_PER_TRIP)
        su_ref[r, :] = (s1_ref[r, :] * _silu(p_ref[r, :])).astype(bf16)
    _for_rows(tm, ROWS_PER_TRIP, go)


def _merge(p_ref, su_ref, w2_ref, m_ref, half, first):
    cols = slice(half * COL, (half + 1) * COL)
    y = jnp.dot(su_ref[...], w2_ref[...], preferred_element_type=f32)
    g = _sigmoid(p_ref[...]) * y
    if first:
        m_ref[:, cols] = g
    else:
        m_ref[:, cols] += g


def _m_to_lhs(m_ref, lhs_ref, tm):
    def go(r0):
        r = pl.ds(r0, ROWS_PER_TRIP)
        lhs_ref[r, :] = m_ref[r, :].astype(bf16)
    _for_rows(tm, ROWS_PER_TRIP, go)


def _post_norm_residual(y0_ref, y1_ref, vd_ref, x_ref, tm):
    rc = min(tm, ROWS_PER_NORM_TRIP)
    def go(r0):
        r = pl.ds(r0, rc)
        y0 = y0_ref[r, :]
        y1 = y1_ref[r, :]
        ss = jnp.sum(y0 * y0, axis=-1, keepdims=True) + jnp.sum(y1 * y1, axis=-1, keepdims=True)
        inv = lax.rsqrt(ss / D_MODEL + EPS)
        x_ref[r, 0:COL] = x_ref[r, 0:COL] + y0 * inv * vd_ref[1:2, 0:COL]
        x_ref[r, COL:D_MODEL] = x_ref[r, COL:D_MODEL] + y1 * inv * vd_ref[1:2, COL:D_MODEL]
    _for_rows(tm, rc, go)


def _ln_rows(src_ref, dst_ref, g, b, tm, swish):
    rc = min(tm, ROWS_PER_NORM_TRIP)
    def go(r0):
        r = pl.ds(r0, rc)
        y = _layernorm_rows(src_ref[r, :], g, b)
        dst_ref[r, :] = _silu(y) if swish else y
    _for_rows(tm, rc, go)


def _run_tile_layer(weights, x_ref, vd_ref, out_ref,
                    lhs_ref, p_ref, s1_ref, su_ref, m_ref, wbuf_ref, w2buf_ref, wsem, w2sem,
                    tm, branch_steps, proj_dst):
    w_in_hbm, w_out_hbm = weights[0], weights[1]
    w_br_hbm = weights[2:]
    i, l = pl.program_id(0), pl.program_id(1)
    n_tiles, depth = pl.num_programs(0), pl.num_programs(1)
    is_first = jnp.logical_and(i == 0, l == 0)
    is_last = jnp.logical_and(i == n_tiles - 1, l == depth - 1)
    next_l = jnp.where(l + 1 < depth, l + 1, 0)

    ring_base = lax.rem((i * depth + l) * N_BLOCKS, N_SLOTS)

    def slot_of(k):
        return lax.rem(ring_base + k, N_SLOTS)

    def w_copy(layer, k):
        slot = slot_of(k)
        kk = k % N_BLOCKS
        src, kk = (w_in_hbm, kk) if kk < N_IN_BLOCKS else (w_out_hbm, kk - N_IN_BLOCKS)
        return pltpu.make_async_copy(src.at[layer, :, pl.ds(kk * COL, COL)], wbuf_ref.at[slot], wsem.at[slot])

    def w2_copy(j):
        slot = j % N_W2_SLOTS
        return pltpu.make_async_copy(w_br_hbm[j // 2].at[l, :, pl.ds((j % 2) * COL, COL)],
                                     w2buf_ref.at[slot], w2sem.at[slot])

    @pl.when(is_first)
    def _():
        for k in range(LOOKAHEAD):
            w_copy(l, k).start()

    _start_layer(l, x_ref, vd_ref, out_ref, lhs_ref, tm)

    for k in range(N_BLOCKS):
        w_copy(l, k).wait()
        if k + LOOKAHEAD < N_BLOCKS:
            w_copy(l, k + LOOKAHEAD).start()
        else:
            @pl.when(jnp.logical_not(is_last))
            def _(k=k):
                w_copy(next_l, k + LOOKAHEAD).start()
        if k == FIRST_MERGE_BLOCK - 1:
            w2_copy(0).start()

        proj = jnp.dot(lhs_ref[...], wbuf_ref[slot_of(k)], preferred_element_type=f32)
        if k in proj_dst:
            proj_dst[k][...] = proj
        else:
            p_ref[k % 2] = proj

        if k in branch_steps:
            branch_steps[k]()
        if k in (2, 5, 7):
            _widen(p_ref.at[k % 2], s1_ref, su_ref.at[(2, 5, 7).index(k)], tm)
        if FIRST_MERGE_BLOCK <= k < N_IN_BLOCKS:
            j = k - FIRST_MERGE_BLOCK
            w2_copy(j).wait()
            if j + 1 < N_MERGE:
                w2_copy(j + 1).start()
            _merge(p_ref.at[k % 2], su_ref.at[j // 2], w2buf_ref.at[j % N_W2_SLOTS], m_ref, j % 2, first=j < 2)
        if k == N_IN_BLOCKS - 1:
            _m_to_lhs(m_ref, lhs_ref, tm)
        if k == N_BLOCKS - 1:
            _post_norm_residual(p_ref.at[0], p_ref.at[1], vd_ref, out_ref, tm)


def _conv_rows_prompt(zbuf_ref, cw_ref, ve_ref, dst_ref, r0, rc):
    for cb in range(E // LANES):
        cols = slice(cb * LANES, (cb + 1) * LANES)
        blk = zbuf_ref[pl.ds(r0, rc + CONV_HALO), cols]
        lead = CONV_HALO - (CONV_W - 1)
        y = ve_ref[0:1, cols] + cw_ref[CONV_W - 1:CONV_W, cols] * blk[CONV_HALO:CONV_HALO + rc, :]
        for b in range(SUBLANES):
            acc = None
            for a in range(CONV_HALO // SUBLANES):
                k = SUBLANES * a + b - lead
                if k < 0:
                    continue
                term = cw_ref[k:k + 1, cols] * blk[SUBLANES * a:SUBLANES * a + rc + SUBLANES, :]
                acc = term if acc is None else acc + term
            y = y + acc[b:b + rc, :]
        dst_ref[pl.ds(r0, rc), cols] = y


def _trailing_sum(blk, w):
    rows = blk.shape[0]
    win, lo, span = blk, 0, 1
    while span < w:
        n = rows - (lo + SUBLANES)
        win = win[SUBLANES:SUBLANES + n, :] + win[SUBLANES - span:SUBLANES - span + n, :]
        lo += SUBLANES
        span *= 2
    return win[POOL_HALO - lo:, :]


def _prompt_kernel(x_ref, w_in_hbm, w_out_hbm, w_br_a_hbm, w_br_b_hbm, w_br_c_hbm,
                   vd_ref, ve_ref, cw_ref, ws_ref, bs_ref, pw_ref,
                   out_ref, cst_ref, pst_ref,
                   lhs_ref, p_ref, s1_ref, su_ref, m_ref, wbuf_ref, w2buf_ref, wsem, w2sem,
                   zbuf_ref, cbuf_ref, zhalo_ref, chalo_ref,
                   *, tiles_per_seq):
    tm = TM_PROMPT
    l = pl.program_id(1)
    q = pl.program_id(0) % tiles_per_seq

    def conv_branch():
        @pl.when(q == 0)
        def _():
            zbuf_ref[0:CONV_HALO, :] = jnp.zeros((CONV_HALO, E), f32)

        @pl.when(q > 0)
        def _():
            zbuf_ref[0:CONV_HALO, :] = zhalo_ref[l]

        def glu(r0):
            r = pl.ds(r0, ROWS_PER_TRIP)
            zbuf_ref[pl.ds(CONV_HALO + r0, ROWS_PER_TRIP), :] = p_ref[0, r, :] * _sigmoid(p_ref[1, r, :])
        _for_rows(tm, ROWS_PER_TRIP, glu)

        _for_rows(tm, CHUNK, lambda r0: _conv_rows_prompt(zbuf_ref, cw_ref, ve_ref, s1_ref, r0, CHUNK),
                  rolled=True)
        _ln_rows(s1_ref, s1_ref, ve_ref[1:2, :], ve_ref[2:3, :], tm, swish=True)

        cst_ref[...] = zbuf_ref[CONV_HALO + tm - (CONV_W - 1):CONV_HALO + tm, :]
        zhalo_ref[l] = zbuf_ref[tm:tm + CONV_HALO, :]

    def gmlp_branch():
        _ln_rows(p_ref.at[0], s1_ref, ve_ref[3:4, :], ve_ref[4:5, :], tm, swish=False)
        tril = (lax.broadcasted_iota(jnp.int32, (CHUNK, CHUNK), 0)
                >= lax.broadcasted_iota(jnp.int32, (CHUNK, CHUNK), 1))
        for grp in range(B_GROUPS):
            cols = slice(grp * B_GC, (grp + 1) * B_GC)
            w_g = jnp.where(tril, ws_ref[grp], 0.0).astype(bf16)
            for c in range(tm // CHUNK):
                rows = slice(c * CHUNK, (c + 1) * CHUNK)
                v = s1_ref[rows, cols].astype(bf16)
                mixed = jnp.dot(w_g, v, preferred_element_type=f32) + bs_ref[:, cols]
                s1_ref[rows, cols] = p_ref[1, rows, cols] * mixed

    def pool_branch():
        @pl.when(q == 0)
        def _():
            cbuf_ref[0:POOL_HALO, :] = jnp.zeros((POOL_HALO, E), f32)

        @pl.when(q > 0)
        def _():
            cbuf_ref[0:POOL_HALO, :] = chalo_ref[l]

        def pool(r0, full_windows):
            pos = q * tm + r0 + lax.broadcasted_iota(jnp.int32, (CHUNK, C_GC), 0)
            for gi, w in enumerate(POOL_WINDOWS):
                cols = slice(gi * C_GC, (gi + 1) * C_GC)
                blk = cbuf_ref[pl.ds(r0, POOL_HALO + CHUNK), cols]
                c = blk[POOL_HALO:, :]
                if full_windows:
                    mean = _trailing_sum(blk, w) * (1.0 / w)
                else:
                    mean = _trailing_sum(blk, w) / jnp.minimum(w, pos + 1).astype(f32)
                y = jnp.dot((mean - c).astype(bf16), pw_ref[gi], preferred_element_type=f32)
                s1_ref[pl.ds(r0, CHUNK), cols] = y * ve_ref[5:6, cols]

        def pool_chunk(r0):
            if r0 > 0:
                pool(r0, full_windows=True)
            else:
                pl.when(q == 0)(lambda: pool(r0, full_windows=False))
                pl.when(q > 0)(lambda: pool(r0, full_windows=True))
        assert CHUNK >= POOL_MAX
        _for_rows(tm, CHUNK, pool_chunk)

        pst_ref[...] = cbuf_ref[POOL_HALO + tm - (POOL_MAX - 1):POOL_HALO + tm, :]
        chalo_ref[l] = cbuf_ref[tm:tm + POOL_HALO, :]

    _run_tile_layer((w_in_hbm, w_out_hbm, w_br_a_hbm, w_br_b_hbm, w_br_c_hbm), x_ref, vd_ref, out_ref,
                    lhs_ref, p_ref, s1_ref, su_ref, m_ref, wbuf_ref, w2buf_ref, wsem, w2sem,
                    tm, {1: conv_branch, 4: gmlp_branch, 6: pool_branch},
                    proj_dst={6: cbuf_ref.at[pl.ds(POOL_HALO, tm), :]})


def _sample_kernel(x_ref, w_in_hbm, w_out_hbm, w_br_a_hbm, w_br_b_hbm, w_br_c_hbm,
                   vd_ref, ve_ref, cw_ref, mix_ref, b8_ref, pw_ref, stc_ref, stp_ref,
                   out_ref, cso_ref, pso_ref, v_ref,
                   lhs_ref, p_ref, s1_ref, su_ref, m_ref, wbuf_ref, w2buf_ref, wsem, w2sem):
    tm = TM_SAMPLE
    sq = tm // DEC_SEQ
    cbw = 256

    def slab(t):
        return slice(t * sq, (t + 1) * sq)

    def conv_branch():
        def glu(r0):
            r = pl.ds(r0, ROWS_PER_TRIP)
            m_ref[r, 0:E] = p_ref[0, r, :] * _sigmoid(p_ref[1, r, :])
        _for_rows(tm, ROWS_PER_TRIP, glu)

        def zp(j, cols):
            if j < CONV_W - 1:
                return stc_ref[j, :, cols]
            return m_ref[slab(j - (CONV_W - 1)), cols]

        for j in range(CONV_W - 1):
            cso_ref[j] = zp(j + DEC_SEQ, slice(0, E))
        for cb in range(E // cbw):
            cols = slice(cb * cbw, (cb + 1) * cbw)
            for t in range(DEC_SEQ):
                acc = jnp.broadcast_to(ve_ref[0:1, cols], (sq, cbw))
                for k in range(CONV_W):
                    acc = acc + cw_ref[k:k + 1, cols] * zp(t + k, cols)
                s1_ref[slab(t), cols] = acc
        _ln_rows(s1_ref, s1_ref, ve_ref[1:2, :], ve_ref[2:3, :], tm, swish=True)

    def gmlp_branch():
        _ln_rows(p_ref.at[0], m_ref.at[:, 0:E], ve_ref[3:4, :], ve_ref[4:5, :], tm, swish=False)
        for t in range(DEC_SEQ):
            v_ref[t] = m_ref[slab(t), 0:E]
        for cb in range(E // cbw):
            cols = slice(cb * cbw, (cb + 1) * cbw)
            for t in range(DEC_SEQ):
                acc = jnp.broadcast_to(b8_ref[t:t + 1, cols], (sq, cbw))
                for s in range(t + 1):
                    acc = acc + mix_ref[t, s:s + 1, cols] * m_ref[slab(s), cols]
                s1_ref[slab(t), cols] = p_ref[1, slab(t), cols] * acc

    def pool_branch():
        def zc(j, cols):
            if j < POOL_MAX - 1:
                return stp_ref[j, :, cols]
            return p_ref[0, slab(j - (POOL_MAX - 1)), cols]

        for j in range(POOL_MAX - 1):
            pso_ref[j] = zc(j + DEC_SEQ, slice(0, E))
        for gi, w in enumerate(POOL_WINDOWS):
            cols = slice(gi * C_GC, (gi + 1) * C_GC)
            for t in range(DEC_SEQ):
                c = zc(POOL_MAX - 1 + t, cols)
                win = c
                for j in range(1, w):
                    win = win + zc(POOL_MAX - 1 + t - j, cols)
                s1_ref[slab(t), cols] = win / float(min(w, PAST_LEN + 1)) - c
        for gi in range(len(POOL_WINDOWS)):
            cols = slice(gi * C_GC, (gi + 1) * C_GC)
            y = jnp.dot(s1_ref[:, cols].astype(bf16), pw_ref[gi], preferred_element_type=f32)
            s1_ref[:, cols] = y * ve_ref[5:6, cols]

    _run_tile_layer((w_in_hbm, w_out_hbm, w_br_a_hbm, w_br_b_hbm, w_br_c_hbm), x_ref, vd_ref, out_ref,
                    lhs_ref, p_ref, s1_ref, su_ref, m_ref, wbuf_ref, w2buf_ref, wsem, w2sem,
                    tm, {1: conv_branch, 4: gmlp_branch, 6: pool_branch}, proj_dst={})


def _weight_specs():
    return [pl.BlockSpec(memory_space=pl.ANY)] * 5 + [
        pl.BlockSpec((None, 2, D_MODEL), lambda i, l: (l, 0, 0)),
        pl.BlockSpec((None, 6, E), lambda i, l: (l, 0, 0)),
        pl.BlockSpec((None, CONV_W, E), lambda i, l: (l, 0, 0)),
    ]


def _work_buffers(tm):
    return [
        pltpu.VMEM((tm, D_MODEL), bf16),
        pltpu.VMEM((2, tm, COL), f32),
        pltpu.VMEM((tm, E), f32),
        pltpu.VMEM((N_BRANCH, tm, E), bf16),
        pltpu.VMEM((tm, D_MODEL), f32),
        pltpu.VMEM((N_SLOTS, D_MODEL, COL), bf16),
        pltpu.VMEM((N_W2_SLOTS, E, COL), bf16),
        pltpu.SemaphoreType.DMA((N_SLOTS,)),
        pltpu.SemaphoreType.DMA((N_W2_SLOTS,)),
    ]


def _compiler_params():
    return pltpu.CompilerParams(dimension_semantics=("arbitrary", "arbitrary"),
                                vmem_limit_bytes=VMEM_LIMIT_BYTES)


def _prompt_group(depth, x2d, n_seq, seq_len, weights, vd, ve, cw, ws, bs_full, pw):
    tm = TM_PROMPT
    assert seq_len % tm == 0 and tm % CHUNK == 0
    tiles_per_seq = seq_len // tm
    n_rows = n_seq * seq_len
    grid = (n_rows // tm, depth)
    in_specs = [pl.BlockSpec((tm, D_MODEL), lambda i, l: (i, 0))] + _weight_specs() + [
        pl.BlockSpec((None, B_GROUPS, CHUNK, CHUNK), lambda i, l: (l, 0, 0, 0)),
        pl.BlockSpec((None, CHUNK, E), lambda i, l: (l, 0, 0)),
        pl.BlockSpec((None, len(POOL_WINDOWS), C_GC, C_GC), lambda i, l: (l, 0, 0, 0)),
    ]
    out_specs = [
        pl.BlockSpec((tm, D_MODEL), lambda i, l: (i, 0)),
        pl.BlockSpec((None, None, CONV_W - 1, E), lambda i, l: (l, i, 0, 0)),
        pl.BlockSpec((None, None, POOL_MAX - 1, E), lambda i, l: (l, i, 0, 0)),
    ]
    out_shape = [
        jax.ShapeDtypeStruct((n_rows, D_MODEL), f32),
        jax.ShapeDtypeStruct((depth, n_rows // tm, CONV_W - 1, E), f32),
        jax.ShapeDtypeStruct((depth, n_rows // tm, POOL_MAX - 1, E), f32),
    ]
    scratch = _work_buffers(tm) + [
        pltpu.VMEM((CONV_HALO + tm, E), f32),
        pltpu.VMEM((POOL_HALO + tm, E), f32),
        pltpu.VMEM((depth, CONV_HALO, E), f32),
        pltpu.VMEM((depth, POOL_HALO, E), f32),
    ]
    x_out, conv_tiles, pool_tiles = pl.pallas_call(
        functools.partial(_prompt_kernel, tiles_per_seq=tiles_per_seq),
        grid=grid, in_specs=in_specs, out_specs=out_specs, out_shape=out_shape,
        scratch_shapes=scratch, compiler_params=_compiler_params(),
        name="prompt_group",
    )(x2d, *weights, vd, ve, cw, ws, bs_full, pw)
    last = slice(tiles_per_seq - 1, None, tiles_per_seq)
    return x_out, conv_tiles[:, last], pool_tiles[:, last]


def _sample_group(depth, x_sample, weights, vd, ve, cw, mix, b8, pw, state_conv, state_pool):
    tm = TM_SAMPLE
    n_dec = x_sample.shape[0]
    sq = tm // DEC_SEQ
    assert n_dec % sq == 0
    n_tiles = n_dec // sq
    x2d = jnp.transpose(x_sample.reshape(n_tiles, sq, DEC_SEQ, D_MODEL), (0, 2, 1, 3)).reshape(n_dec * DEC_SEQ, D_MODEL)
    stc = jnp.transpose(state_conv, (0, 2, 1, 3))
    stp = jnp.transpose(state_pool, (0, 2, 1, 3))
    grid = (n_tiles, depth)
    in_specs = [pl.BlockSpec((tm, D_MODEL), lambda i, l: (i, 0))] + _weight_specs() + [
        pl.BlockSpec((None, DEC_SEQ, DEC_SEQ, E), lambda i, l: (l, 0, 0, 0)),
        pl.BlockSpec((None, DEC_SEQ, E), lambda i, l: (l, 0, 0)),
        pl.BlockSpec((None, len(POOL_WINDOWS), C_GC, C_GC), lambda i, l: (l, 0, 0, 0)),
        pl.BlockSpec((None, CONV_W - 1, sq, E), lambda i, l: (l, 0, i, 0)),
        pl.BlockSpec((None, POOL_MAX - 1, sq, E), lambda i, l: (l, 0, i, 0)),
    ]
    out_specs = [
        pl.BlockSpec((tm, D_MODEL), lambda i, l: (i, 0)),
        pl.BlockSpec((None, CONV_W - 1, sq, E), lambda i, l: (l, 0, i, 0)),
        pl.BlockSpec((None, POOL_MAX - 1, sq, E), lambda i, l: (l, 0, i, 0)),
        pl.BlockSpec((None, DEC_SEQ, sq, E), lambda i, l: (l, 0, i, 0)),
    ]
    out_shape = [
        jax.ShapeDtypeStruct((n_dec * DEC_SEQ, D_MODEL), f32),
        jax.ShapeDtypeStruct((depth, CONV_W - 1, n_dec, E), f32),
        jax.ShapeDtypeStruct((depth, POOL_MAX - 1, n_dec, E), f32),
        jax.ShapeDtypeStruct((depth, DEC_SEQ, n_dec, E), f32),
    ]
    x_out, conv_s, pool_s, v_s = pl.pallas_call(
        _sample_kernel,
        grid=grid, in_specs=in_specs, out_specs=out_specs, out_shape=out_shape,
        scratch_shapes=_work_buffers(tm), compiler_params=_compiler_params(),
        name="sample_group",
    )(x2d, *weights, vd, ve, cw, mix, b8, pw, stc, stp)
    x_out = jnp.transpose(x_out.reshape(n_tiles, DEC_SEQ, sq, D_MODEL), (0, 2, 1, 3)).reshape(n_dec, DEC_SEQ, D_MODEL)
    back = lambda a: jnp.transpose(a, (0, 2, 1, 3))
    return x_out, back(conv_s), back(pool_s), back(v_s)


def _sample_mix_tables(gmlp_ws, gmlp_bs):
    mix = jnp.repeat(jnp.transpose(gmlp_ws[:, :, :DEC_SEQ, :DEC_SEQ], (0, 2, 3, 1)), B_GC, axis=-1)
    b8 = jnp.repeat(jnp.transpose(gmlp_bs[:, :, :DEC_SEQ], (0, 2, 1)), B_GC, axis=-1)
    return mix, b8


def kernel(x_prompt, x_sample, state_conv, state_pool, g_pre, w_in, conv_w, conv_b, conv_ln_g, conv_ln_b,
           w_br_a, gmlp_ln_g, gmlp_ln_b, gmlp_ws, gmlp_bs, w_br_b, pool_w, pool_scale, w_br_c, w_out, g_post):
    depth = w_in.shape[0]
    n_seq, seq_len, _ = x_prompt.shape
    n_dec, dec_seq, _ = x_sample.shape
    assert dec_seq == DEC_SEQ and x_prompt.shape[2] == D_MODEL

    weights = tuple(w.astype(bf16) for w in (w_in, w_out, w_br_a, w_br_b, w_br_c))
    vd = jnp.stack([g_pre, g_post], axis=1)
    ve = jnp.stack([conv_b, conv_ln_g, conv_ln_b, gmlp_ln_g, gmlp_ln_b, pool_scale], axis=1)
    bs_full = jnp.repeat(jnp.transpose(gmlp_bs, (0, 2, 1)), B_GC, axis=-1)
    pw = pool_w.astype(bf16)
    mix, b8 = _sample_mix_tables(gmlp_ws, gmlp_bs)

    xp, conv_p, pool_p = _prompt_group(depth, x_prompt.reshape(n_seq * seq_len, D_MODEL), n_seq, seq_len,
                                       weights, vd, ve, conv_w, gmlp_ws, bs_full, pw)
    xs, conv_s, pool_s, v_s = _sample_group(depth, x_sample, weights, vd, ve, conv_w, mix, b8, pw,
                                            state_conv, state_pool)
    return (xp.reshape(n_seq, seq_len, D_MODEL), xs, conv_p, pool_p, conv_s, pool_s, v_s)
```

```python
import functools

import jax
import jax.numpy as jnp
import numpy as np
from jax import lax
from jax.experimental import pallas as pl
from jax.experimental.pallas import tpu as pltpu

D_MODEL = 2048
E = 1024
N_BRANCH = 3
CONV_W = 31
CONV_HALO = 32
SUBLANES = 8
LANES = 128
CHUNK = 128
B_GROUPS = 8
B_GC = E // B_GROUPS
POOL_WINDOWS = (2, 4, 8, 16)
C_GC = E // len(POOL_WINDOWS)
POOL_MAX = 16
POOL_HALO = 32
DEC_SEQ = 8
PAST_LEN = 16384
EPS = 1e-6
COL = 1024
N_IN_BLOCKS = (8 * E + N_BRANCH * D_MODEL) // COL
N_BLOCKS = N_IN_BLOCKS + D_MODEL // COL
FIRST_MERGE_BLOCK = 8
N_MERGE = 2 * N_BRANCH
LOOKAHEAD = 2
N_SLOTS = LOOKAHEAD + 1
N_W2_SLOTS = 2

TM_PROMPT = 512
TM_SAMPLE = 256
VMEM_LIMIT_BYTES = 60 * 1024 * 1024

f32 = jnp.float32
bf16 = jnp.bfloat16


ROWS_PER_TRIP = 64
ROWS_PER_NORM_TRIP = 512


def _for_rows(n_rows, rc, fn, rolled=False):
    if not rolled:
        for r0 in range(0, n_rows, rc):
            fn(r0)
        return

    def body(c, carry):
        fn(pl.multiple_of(c * rc, rc))
        return carry
    lax.fori_loop(0, n_rows // rc, body, 0)


def _sigmoid(x):
    return 0.5 * jnp.tanh(0.5 * x) + 0.5


def _silu(x):
    h = 0.5 * x
    return h * (jnp.tanh(h) + 1.0)


def _layernorm_rows(y, g, b):
    mu = jnp.mean(y, axis=-1, keepdims=True)
    yc = y - mu
    var = jnp.mean(yc * yc, axis=-1, keepdims=True)
    return yc * lax.rsqrt(var + EPS) * g + b


def _start_layer(l, x_ref, vd_ref, out_ref, lhs_ref, tm):
    @pl.when(l == 0)
    def _():
        def load(r0):
            out_ref[pl.ds(r0, ROWS_PER_TRIP), :] = x_ref[pl.ds(r0, ROWS_PER_TRIP), :]
        _for_rows(tm, ROWS_PER_TRIP, load)

    rc = min(tm, ROWS_PER_NORM_TRIP)
    def norm(r0):
        r = pl.ds(r0, rc)
        x = out_ref[r, :]
        ms = jnp.mean(x * x, axis=-1, keepdims=True)
        lhs_ref[r, :] = (x * lax.rsqrt(ms + EPS) * vd_ref[0:1, :]).astype(bf16)
    _for_rows(tm, rc, norm)


def _widen(p_ref, s1_ref, su_ref, tm):
    def go(r0):
        r = pl.ds(r0, ROWS_PER_TRIP)
        su_ref[r, :] = (s1_ref[r, :] * _silu(p_ref[r, :])).astype(bf16)
    _for_rows(tm, ROWS_PER_TRIP, go)


def _merge(p_ref, su_ref, w2_ref, m_ref, half, first):
    cols = slice(half * COL, (half + 1) * COL)
    y = jnp.dot(su_ref[...], w2_ref[...], preferred_element_type=f32)
    g = _sigmoid(p_ref[...]) * y
    if first:
        m_ref[:, cols] = g
    else:
        m_ref[:, cols] += g


def _m_to_lhs(m_ref, lhs_ref, tm):
    def go(r0):
        r = pl.ds(r0, ROWS_PER_TRIP)
        lhs_ref[r, :] = m_ref[r, :].astype(bf16)
    _for_rows(tm, ROWS_PER_TRIP, go)


def _post_norm_residual(y0_ref, y1_ref, vd_ref, x_ref, tm):
    rc = min(tm, ROWS_PER_NORM_TRIP)
    def go(r0):
        r = pl.ds(r0, rc)
        y0 = y0_ref[r, :]
        y1 = y1_ref[r, :]
        ss = jnp.sum(y0 * y0, axis=-1, keepdims=True) + jnp.sum(y1 * y1, axis=-1, keepdims=True)
        inv = lax.rsqrt(ss / D_MODEL + EPS)
        x_ref[r, 0:COL] = x_ref[r, 0:COL] + y0 * inv * vd_ref[1:2, 0:COL]
        x_ref[r, COL:D_MODEL] = x_ref[r, COL:D_MODEL] + y1 * inv * vd_ref[1:2, COL:D_MODEL]
    _for_rows(tm, rc, go)


def _ln_rows(src_ref, dst_ref, g, b, tm, swish):
    rc = min(tm, ROWS_PER_NORM_TRIP)
    def go(r0):
        r = pl.ds(r0, rc)
        y = _layernorm_rows(src_ref[r, :], g, b)
        dst_ref[r, :] = _silu(y) if swish else y
    _for_rows(tm, rc, go)


def _run_tile_layer(weights, x_ref, vd_ref, out_ref,
                    lhs_ref, p_ref, s1_ref, su_ref, m_ref, wbuf_ref, w2buf_ref, wsem, w2sem,
                    tm, branch_steps, proj_dst):
    w_in_hbm, w_out_hbm = weights[0], weights[1]
    w_br_hbm = weights[2:]
    i, l = pl.program_id(0), pl.program_id(1)
    n_tiles, depth = pl.num_programs(0), pl.num_programs(1)
    is_first = jnp.logical_and(i == 0, l == 0)
    is_last = jnp.logical_and(i == n_tiles - 1, l == depth - 1)
    next_l = jnp.where(l + 1 < depth, l + 1, 0)

    ring_base = lax.rem((i * depth + l) * N_BLOCKS, N_SLOTS)

    def slot_of(k):
        return lax.rem(ring_base + k, N_SLOTS)

    def w_copy(layer, k):
        slot = slot_of(k)
        kk = k % N_BLOCKS
        src, kk = (w_in_hbm, kk) if kk < N_IN_BLOCKS else (w_out_hbm, kk - N_IN_BLOCKS)
        return pltpu.make_async_copy(src.at[layer, :, pl.ds(kk * COL, COL)], wbuf_ref.at[slot], wsem.at[slot])

    def w2_copy(j):
        slot = j % N_W2_SLOTS
        return pltpu.make_async_copy(w_br_hbm[j // 2].at[l, :, pl.ds((j % 2) * COL, COL)],
                                     w2buf_ref.at[slot], w2sem.at[slot])

    @pl.when(is_first)
    def _():
        for k in range(N_SLOTS):
            w_copy(l, k).start()

    w_copy(l, 0).wait()
    _start_layer(l, x_ref, vd_ref, out_ref, lhs_ref, tm)

    for k in range(N_BLOCKS):
        if k == FIRST_MERGE_BLOCK - 1:
            w2_copy(0).start()

        proj = jnp.dot(lhs_ref[...], wbuf_ref[slot_of(k)], preferred_element_type=f32)
        if k in proj_dst:
            proj_dst[k][...] = proj
        else:
            p_ref[k % 2] = proj

        if k + 1 < N_BLOCKS:
            w_copy(l, k + 1).wait()
        if k + N_SLOTS < N_BLOCKS:
            w_copy(l, k + N_SLOTS).start()
        else:
            @pl.when(jnp.logical_not(is_last))
            def _(k=k):
                w_copy(next_l, k + N_SLOTS).start()

        if k in branch_steps:
            branch_steps[k]()
        if k in (2, 5, 7):
            _widen(p_ref.at[k % 2], s1_ref, su_ref.at[(2, 5, 7).index(k)], tm)
        if FIRST_MERGE_BLOCK <= k < N_IN_BLOCKS:
            j = k - FIRST_MERGE_BLOCK
            w2_copy(j).wait()
            if j + 1 < N_MERGE:
                w2_copy(j + 1).start()
            _merge(p_ref.at[k % 2], su_ref.at[j // 2], w2buf_ref.at[j % N_W2_SLOTS], m_ref, j % 2, first=j < 2)
        if k == N_IN_BLOCKS - 1:
            _m_to_lhs(m_ref, lhs_ref, tm)
        if k == N_BLOCKS - 1:
            _post_norm_residual(p_ref.at[0], p_ref.at[1], vd_ref, out_ref, tm)


def _conv_rows_prompt(zbuf_ref, cw_ref, ve_ref, dst_ref, r0, rc):
    for cb in range(E // LANES):
        cols = slice(cb * LANES, (cb + 1) * LANES)
        blk = zbuf_ref[pl.ds(r0, rc + CONV_HALO), cols]
        lead = CONV_HALO - (CONV_W - 1)
        y = ve_ref[0:1, cols] + cw_ref[CONV_W - 1:CONV_W, cols] * blk[CONV_HALO:CONV_HALO + rc, :]
        for b in range(SUBLANES):
            acc = None
            for a in range(CONV_HALO // SUBLANES):
                k = SUBLANES * a + b - lead
                if k < 0:
                    continue
                term = cw_ref[k:k + 1, cols] * blk[SUBLANES * a:SUBLANES * a + rc + SUBLANES, :]
                acc = term if acc is None else acc + term
            y = y + acc[b:b + rc, :]
        dst_ref[pl.ds(r0, rc), cols] = y


def _trailing_sum(blk, w):
    rows = blk.shape[0]
    win, lo, span = blk, 0, 1
    while span < w:
        n = rows - (lo + SUBLANES)
        win = win[SUBLANES:SUBLANES + n, :] + win[SUBLANES - span:SUBLANES - span + n, :]
        lo += SUBLANES
        span *= 2
    return win[POOL_HALO - lo:, :]


def _prompt_kernel(x_ref, w_in_hbm, w_out_hbm, w_br_a_hbm, w_br_b_hbm, w_br_c_hbm,
                   vd_ref, ve_ref, cw_ref, ws_ref, bs_ref, pw_ref,
                   out_ref, cst_ref, pst_ref,
                   lhs_ref, p_ref, s1_ref, su_ref, m_ref, wbuf_ref, w2buf_ref, wsem, w2sem,
                   zbuf_ref, cbuf_ref, zhalo_ref, chalo_ref,
                   *, tiles_per_seq):
    tm = TM_PROMPT
    l = pl.program_id(1)
    q = pl.program_id(0) % tiles_per_seq

    def conv_branch():
        @pl.when(q == 0)
        def _():
            zbuf_ref[0:CONV_HALO, :] = jnp.zeros((CONV_HALO, E), f32)

        @pl.when(q > 0)
        def _():
            zbuf_ref[0:CONV_HALO, :] = zhalo_ref[l]

        def glu(r0):
            r = pl.ds(r0, ROWS_PER_TRIP)
            zbuf_ref[pl.ds(CONV_HALO + r0, ROWS_PER_TRIP), :] = p_ref[0, r, :] * _sigmoid(p_ref[1, r, :])
        _for_rows(tm, ROWS_PER_TRIP, glu)

        _for_rows(tm, CHUNK, lambda r0: _conv_rows_prompt(zbuf_ref, cw_ref, ve_ref, s1_ref, r0, CHUNK),
                  rolled=True)
        _ln_rows(s1_ref, s1_ref, ve_ref[1:2, :], ve_ref[2:3, :], tm, swish=True)

        cst_ref[...] = zbuf_ref[CONV_HALO + tm - (CONV_W - 1):CONV_HALO + tm, :]
        zhalo_ref[l] = zbuf_ref[tm:tm + CONV_HALO, :]

    def gmlp_branch():
        _ln_rows(p_ref.at[0], s1_ref, ve_ref[3:4, :], ve_ref[4:5, :], tm, swish=False)
        tril = (lax.broadcasted_iota(jnp.int32, (CHUNK, CHUNK), 0)
                >= lax.broadcasted_iota(jnp.int32, (CHUNK, CHUNK), 1))
        for grp in range(B_GROUPS):
            cols = slice(grp * B_GC, (grp + 1) * B_GC)
            w_g = jnp.where(tril, ws_ref[grp], 0.0).astype(bf16)
            for c in range(tm // CHUNK):
                rows = slice(c * CHUNK, (c + 1) * CHUNK)
                v = s1_ref[rows, cols].astype(bf16)
                mixed = jnp.dot(w_g, v, preferred_element_type=f32) + bs_ref[:, cols]
                s1_ref[rows, cols] = p_ref[1, rows, cols] * mixed

    def pool_branch():
        @pl.when(q == 0)
        def _():
            cbuf_ref[0:POOL_HALO, :] = jnp.zeros((POOL_HALO, E), f32)

        @pl.when(q > 0)
        def _():
            cbuf_ref[0:POOL_HALO, :] = chalo_ref[l]

        def pool(r0, full_windows):
            pos = q * tm + r0 + lax.broadcasted_iota(jnp.int32, (CHUNK, C_GC), 0)
            for gi, w in enumerate(POOL_WINDOWS):
                cols = slice(gi * C_GC, (gi + 1) * C_GC)
                blk = cbuf_ref[pl.ds(r0, POOL_HALO + CHUNK), cols]
                c = blk[POOL_HALO:, :]
                if full_windows:
                    mean = _trailing_sum(blk, w) * (1.0 / w)
                else:
                    mean = _trailing_sum(blk, w) / jnp.minimum(w, pos + 1).astype(f32)
                y = jnp.dot((mean - c).astype(bf16), pw_ref[gi], preferred_element_type=f32)
                s1_ref[pl.ds(r0, CHUNK), cols] = y * ve_ref[5:6, cols]

        def pool_chunk(r0):
            if r0 > 0:
                pool(r0, full_windows=True)
            else:
                pl.when(q == 0)(lambda: pool(r0, full_windows=False))
                pl.when(q > 0)(lambda: pool(r0, full_windows=True))
        assert CHUNK >= POOL_MAX
        _for_rows(tm, CHUNK, pool_chunk)

        pst_ref[...] = cbuf_ref[POOL_HALO + tm - (POOL_MAX - 1):POOL_HALO + tm, :]
        chalo_ref[l] = cbuf_ref[tm:tm + POOL_HALO, :]

    _run_tile_layer((w_in_hbm, w_out_hbm, w_br_a_hbm, w_br_b_hbm, w_br_c_hbm), x_ref, vd_ref, out_ref,
                    lhs_ref, p_ref, s1_ref, su_ref, m_ref, wbuf_ref, w2buf_ref, wsem, w2sem,
                    tm, {1: conv_branch, 4: gmlp_branch, 6: pool_branch},
                    proj_dst={6: cbuf_ref.at[pl.ds(POOL_HALO, tm), :]})


def _sample_kernel(x_ref, w_in_hbm, w_out_hbm, w_br_a_hbm, w_br_b_hbm, w_br_c_hbm,
                   vd_ref, ve_ref, cw_ref, mix_ref, b8_ref, pw_ref, stc_ref, stp_ref,
                   out_ref, cso_ref, pso_ref, v_ref,
                   lhs_ref, p_ref, s1_ref, su_ref, m_ref, wbuf_ref, w2buf_ref, wsem, w2sem):
    tm = TM_SAMPLE
    sq = tm // DEC_SEQ
    cbw = 256

    def slab(t):
        return slice(t * sq, (t + 1) * sq)

    def conv_branch():
        def glu(r0):
            r = pl.ds(r0, ROWS_PER_TRIP)
            m_ref[r, 0:E] = p_ref[0, r, :] * _sigmoid(p_ref[1, r, :])
        _for_rows(tm, ROWS_PER_TRIP, glu)

        def zp(j, cols):
            if j < CONV_W - 1:
                return stc_ref[j, :, cols]
            return m_ref[slab(j - (CONV_W - 1)), cols]

        for j in range(CONV_W - 1):
            cso_ref[j] = zp(j + DEC_SEQ, slice(0, E))
        for cb in range(E // cbw):
            cols = slice(cb * cbw, (cb + 1) * cbw)
            for t in range(DEC_SEQ):
                acc = jnp.broadcast_to(ve_ref[0:1, cols], (sq, cbw))
                for k in range(CONV_W):
                    acc = acc + cw_ref[k:k + 1, cols] * zp(t + k, cols)
                s1_ref[slab(t), cols] = acc
        _ln_rows(s1_ref, s1_ref, ve_ref[1:2, :], ve_ref[2:3, :], tm, swish=True)

    def gmlp_branch():
        _ln_rows(p_ref.at[0], m_ref.at[:, 0:E], ve_ref[3:4, :], ve_ref[4:5, :], tm, swish=False)
        for t in range(DEC_SEQ):
            v_ref[t] = m_ref[slab(t), 0:E]
        for cb in range(E // cbw):
            cols = slice(cb * cbw, (cb + 1) * cbw)
            for t in range(DEC_SEQ):
                acc = jnp.broadcast_to(b8_ref[t:t + 1, cols], (sq, cbw))
                for s in range(t + 1):
                    acc = acc + mix_ref[t, s:s + 1, cols] * m_ref[slab(s), cols]
                s1_ref[slab(t), cols] = p_ref[1, slab(t), cols] * acc

    def pool_branch():
        def zc(j, cols):
            if j < POOL_MAX - 1:
                return stp_ref[j, :, cols]
            return p_ref[0, slab(j - (POOL_MAX - 1)), cols]

        for j in range(POOL_MAX - 1):
            pso_ref[j] = zc(j + DEC_SEQ, slice(0, E))
        for gi, w in enumerate(POOL_WINDOWS):
            cols = slice(gi * C_GC, (gi + 1) * C_GC)
            for t in range(DEC_SEQ):
                c = zc(POOL_MAX - 1 + t, cols)
                win = c
                for j in range(1, w):
                    win = win + zc(POOL_MAX - 1 + t - j, cols)
                s1_ref[slab(t), cols] = win / float(min(w, PAST_LEN + 1)) - c
        for gi in range(len(POOL_WINDOWS)):
            cols = slice(gi * C_GC, (gi + 1) * C_GC)
            y = jnp.dot(s1_ref[:, cols].astype(bf16), pw_ref[gi], preferred_element_type=f32)
            s1_ref[:, cols] = y * ve_ref[5:6, cols]

    _run_tile_layer((w_in_hbm, w_out_hbm, w_br_a_hbm, w_br_b_hbm, w_br_c_hbm), x_ref, vd_ref, out_ref,
                    lhs_ref, p_ref, s1_ref, su_ref, m_ref, wbuf_ref, w2buf_ref, wsem, w2sem,
                    tm, {1: conv_branch, 4: gmlp_branch, 6: pool_branch}, proj_dst={})


def _weight_specs():
    return [pl.BlockSpec(memory_space=pl.ANY)] * 5 + [
        pl.BlockSpec((None, 2, D_MODEL), lambda i, l: (l, 0, 0)),
        pl.BlockSpec((None, 6, E), lambda i, l: (l, 0, 0)),
        pl.BlockSpec((None, CONV_W, E), lambda i, l: (l, 0, 0)),
    ]


def _work_buffers(tm):
    return [
        pltpu.VMEM((tm, D_MODEL), bf16),
        pltpu.VMEM((2, tm, COL), f32),
        pltpu.VMEM((tm, E), f32),
        pltpu.VMEM((N_BRANCH, tm, E), bf16),
        pltpu.VMEM((tm, D_MODEL), f32),
        pltpu.VMEM((N_SLOTS, D_MODEL, COL), bf16),
        pltpu.VMEM((N_W2_SLOTS, E, COL), bf16),
        pltpu.SemaphoreType.DMA((N_SLOTS,)),
        pltpu.SemaphoreType.DMA((N_W2_SLOTS,)),
    ]


def _compiler_params():
    return pltpu.CompilerParams(dimension_semantics=("arbitrary", "arbitrary"),
                                vmem_limit_bytes=VMEM_LIMIT_BYTES)


def _prompt_group(depth, x2d, n_seq, seq_len, weights, vd, ve, cw, ws, bs_full, pw):
    tm = TM_PROMPT
    assert seq_len % tm == 0 and tm % CHUNK == 0
    tiles_per_seq = seq_len // tm
    n_rows = n_seq * seq_len
    grid = (n_rows // tm, depth)
    in_specs = [pl.BlockSpec((tm, D_MODEL), lambda i, l: (i, 0))] + _weight_specs() + [
        pl.BlockSpec((None, B_GROUPS, CHUNK, CHUNK), lambda i, l: (l, 0, 0, 0)),
        pl.BlockSpec((None, CHUNK, E), lambda i, l: (l, 0, 0)),
        pl.BlockSpec((None, len(POOL_WINDOWS), C_GC, C_GC), lambda i, l: (l, 0, 0, 0)),
    ]
    out_specs = [
        pl.BlockSpec((tm, D_MODEL), lambda i, l: (i, 0)),
        pl.BlockSpec((None, None, CONV_W - 1, E), lambda i, l: (l, i, 0, 0)),
        pl.BlockSpec((None, None, POOL_MAX - 1, E), lambda i, l: (l, i, 0, 0)),
    ]
    out_shape = [
        jax.ShapeDtypeStruct((n_rows, D_MODEL), f32),
        jax.ShapeDtypeStruct((depth, n_rows // tm, CONV_W - 1, E), f32),
        jax.ShapeDtypeStruct((depth, n_rows // tm, POOL_MAX - 1, E), f32),
    ]
    scratch = _work_buffers(tm) + [
        pltpu.VMEM((CONV_HALO + tm, E), f32),
        pltpu.VMEM((POOL_HALO + tm, E), f32),
        pltpu.VMEM((depth, CONV_HALO, E), f32),
        pltpu.VMEM((depth, POOL_HALO, E), f32),
    ]
    x_out, conv_tiles, pool_tiles = pl.pallas_call(
        functools.partial(_prompt_kernel, tiles_per_seq=tiles_per_seq),
        grid=grid, in_specs=in_specs, out_specs=out_specs, out_shape=out_shape,
        scratch_shapes=scratch, compiler_params=_compiler_params(),
        name="prompt_group",
    )(x2d, *weights, vd, ve, cw, ws, bs_full, pw)
    last = slice(tiles_per_seq - 1, None, tiles_per_seq)
    return x_out, conv_tiles[:, last], pool_tiles[:, last]


def _sample_group(depth, x_sample, weights, vd, ve, cw, mix, b8, pw, state_conv, state_pool):
    tm = TM_SAMPLE
    n_dec = x_sample.shape[0]
    sq = tm // DEC_SEQ
    assert n_dec % sq == 0
    n_tiles = n_dec // sq
    x2d = jnp.transpose(x_sample.reshape(n_tiles, sq, DEC_SEQ, D_MODEL), (0, 2, 1, 3)).reshape(n_dec * DEC_SEQ, D_MODEL)
    stc = jnp.transpose(state_conv, (0, 2, 1, 3))
    stp = jnp.transpose(state_pool, (0, 2, 1, 3))
    grid = (n_tiles, depth)
    in_specs = [pl.BlockSpec((tm, D_MODEL), lambda i, l: (i, 0))] + _weight_specs() + [
        pl.BlockSpec((None, DEC_SEQ, DEC_SEQ, E), lambda i, l: (l, 0, 0, 0)),
        pl.BlockSpec((None, DEC_SEQ, E), lambda i, l: (l, 0, 0)),
        pl.BlockSpec((None, len(POOL_WINDOWS), C_GC, C_GC), lambda i, l: (l, 0, 0, 0)),
        pl.BlockSpec((None, CONV_W - 1, sq, E), lambda i, l: (l, 0, i, 0)),
        pl.BlockSpec((None, POOL_MAX - 1, sq, E), lambda i, l: (l, 0, i, 0)),
    ]
    out_specs = [
        pl.BlockSpec((tm, D_MODEL), lambda i, l: (i, 0)),
        pl.BlockSpec((None, CONV_W - 1, sq, E), lambda i, l: (l, 0, i, 0)),
        pl.BlockSpec((None, POOL_MAX - 1, sq, E), lambda i, l: (l, 0, i, 0)),
        pl.BlockSpec((None, DEC_SEQ, sq, E), lambda i, l: (l, 0, i, 0)),
    ]
    out_shape = [
        jax.ShapeDtypeStruct((n_dec * DEC_SEQ, D_MODEL), f32),
        jax.ShapeDtypeStruct((depth, CONV_W - 1, n_dec, E), f32),
        jax.ShapeDtypeStruct((depth, POOL_MAX - 1, n_dec, E), f32),
        jax.ShapeDtypeStruct((depth, DEC_SEQ, n_dec, E), f32),
    ]
    x_out, conv_s, pool_s, v_s = pl.pallas_call(
        _sample_kernel,
        grid=grid, in_specs=in_specs, out_specs=out_specs, out_shape=out_shape,
        scratch_shapes=_work_buffers(tm), compiler_params=_compiler_params(),
        name="sample_group",
    )(x2d, *weights, vd, ve, cw, mix, b8, pw, stc, stp)
    x_out = jnp.transpose(x_out.reshape(n_tiles, DEC_SEQ, sq, D_MODEL), (0, 2, 1, 3)).reshape(n_dec, DEC_SEQ, D_MODEL)
    back = lambda a: jnp.transpose(a, (0, 2, 1, 3))
    return x_out, back(conv_s), back(pool_s), back(v_s)


def _sample_mix_tables(gmlp_ws, gmlp_bs):
    mix = jnp.repeat(jnp.transpose(gmlp_ws[:, :, :DEC_SEQ, :DEC_SEQ], (0, 2, 3, 1)), B_GC, axis=-1)
    b8 = jnp.repeat(jnp.transpose(gmlp_bs[:, :, :DEC_SEQ], (0, 2, 1)), B_GC, axis=-1)
    return mix, b8


def kernel(x_prompt, x_sample, state_conv, state_pool, g_pre, w_in, conv_w, conv_b, conv_ln_g, conv_ln_b,
           w_br_a, gmlp_ln_g, gmlp_ln_b, gmlp_ws, gmlp_bs, w_br_b, pool_w, pool_scale, w_br_c, w_out, g_post):
    depth = w_in.shape[0]
    n_seq, seq_len, _ = x_prompt.shape
    n_dec, dec_seq, _ = x_sample.shape
    assert dec_seq == DEC_SEQ and x_prompt.shape[2] == D_MODEL

    weights = tuple(w.astype(bf16) for w in (w_in, w_out, w_br_a, w_br_b, w_br_c))
    vd = jnp.stack([g_pre, g_post], axis=1)
    ve = jnp.stack([conv_b, conv_ln_g, conv_ln_b, gmlp_ln_g, gmlp_ln_b, pool_scale], axis=1)
    bs_full = jnp.repeat(jnp.transpose(gmlp_bs, (0, 2, 1)), B_GC, axis=-1)
    pw = pool_w.astype(bf16)
    mix, b8 = _sample_mix_tables(gmlp_ws, gmlp_bs)

    xp, conv_p, pool_p = _prompt_group(depth, x_prompt.reshape(n_seq * seq_len, D_MODEL), n_seq, seq_len,
                                       weights, vd, ve, conv_w, gmlp_ws, bs_full, pw)
    xs, conv_s, pool_s, v_s = _sample_group(depth, x_sample, weights, vd, ve, conv_w, mix, b8, pw,
                                            state_conv, state_pool)
    return (xp.reshape(n_seq, seq_len, D_MODEL), xs, conv_p, pool_p, conv_s, pool_s, v_s)
```

```python
import functools

import jax
import jax.numpy as jnp
import numpy as np
from jax import lax
from jax.experimental import pallas as pl
from jax.experimental.pallas import tpu as pltpu

D_MODEL = 2048
E = 1024
N_BRANCH = 3
CONV_W = 31
CONV_HALO = 32
SUBLANES = 8
LANES = 128
CHUNK = 128
B_GROUPS = 8
B_GC = E // B_GROUPS
POOL_WINDOWS = (2, 4, 8, 16)
C_GC = E // len(POOL_WINDOWS)
POOL_MAX = 16
POOL_HALO = 32
DEC_SEQ = 8
PAST_LEN = 16384
EPS = 1e-6
COL = 1024
N_IN_BLOCKS = (8 * E + N_BRANCH * D_MODEL) // COL
N_BLOCKS = N_IN_BLOCKS + D_MODEL // COL
FIRST_MERGE_BLOCK = 8
N_MERGE = 2 * N_BRANCH
LOOKAHEAD = 2
N_SLOTS = LOOKAHEAD + 1
N_W2_SLOTS = 2

TM_PROMPT = 512
TM_SAMPLE = 256
VMEM_LIMIT_BYTES = 60 * 1024 * 1024

f32 = jnp.float32
bf16 = jnp.bfloat16


ROWS_PER_TRIP = 64
ROWS_PER_NORM_TRIP = 512


def _for_rows(n_rows, rc, fn, rolled=False):
    if not rolled:
        for r0 in range(0, n_rows, rc):
            fn(r0)
        return

    def body(c, carry):
        fn(pl.multiple_of(c * rc, rc))
        return carry
    lax.fori_loop(0, n_rows // rc, body, 0)


def _sigmoid(x):
    return 0.5 * jnp.tanh(0.5 * x) + 0.5


def _silu(x):
    h = 0.5 * x
    return h * (jnp.tanh(h) + 1.0)


def _layernorm_rows(y, g, b):
    mu = jnp.mean(y, axis=-1, keepdims=True)
    yc = y - mu
    var = jnp.mean(yc * yc, axis=-1, keepdims=True)
    return yc * lax.rsqrt(var + EPS) * g + b


def _start_layer(l, x_ref, vd_ref, out_ref, lhs_ref, tm):
    @pl.when(l == 0)
    def _():
        def load(r0):
            out_ref[pl.ds(r0, ROWS_PER_TRIP), :] = x_ref[pl.ds(r0, ROWS_PER_TRIP), :]
        _for_rows(tm, ROWS_PER_TRIP, load)

    rc = min(tm, ROWS_PER_NORM_TRIP)
    def norm(r0):
        r = pl.ds(r0, rc)
        x = out_ref[r, :]
        ms = jnp.mean(x * x, axis=-1, keepdims=True)
        lhs_ref[r, :] = (x * lax.rsqrt(ms + EPS) * vd_ref[0:1, :]).astype(bf16)
    _for_rows(tm, rc, norm)


def _widen(p_ref, s1_ref, su_ref, tm):
    def go(r0):
        r = pl.ds(r0, ROWS_PER_TRIP)
        su_ref[r, :] = (s1_ref[r, :] * _silu(p_ref[r, :])).astype(bf16)
    _for_rows(tm, ROWS_PER_TRIP, go)


def _merge(p_ref, su_ref, w2_ref, m_ref, half, first):
    cols = slice(half * COL, (half + 1) * COL)
    y = jnp.dot(su_ref[...], w2_ref[...], preferred_element_type=f32)
    g = _sigmoid(p_ref[...]) * y
    if first:
        m_ref[:, cols] = g
    else:
        m_ref[:, cols] += g


def _m_to_lhs(m_ref, lhs_ref, tm):
    def go(r0):
        r = pl.ds(r0, ROWS_PER_TRIP)
        lhs_ref[r, :] = m_ref[r, :].astype(bf16)
    _for_rows(tm, ROWS_PER_TRIP, go)


def _post_norm_residual(y0_ref, y1_ref, vd_ref, x_ref, tm):
    rc = min(tm, ROWS_PER_NORM_TRIP)
    def go(r0):
        r = pl.ds(r0, rc)
        y0 = y0_ref[r, :]
        y1 = y1_ref[r, :]
        ss = jnp.sum(y0 * y0, axis=-1, keepdims=True) + jnp.sum(y1 * y1, axis=-1, keepdims=True)
        inv = lax.rsqrt(ss / D_MODEL + EPS)
        x_ref[r, 0:COL] = x_ref[r, 0:COL] + y0 * inv * vd_ref[1:2, 0:COL]
        x_ref[r, COL:D_MODEL] = x_ref[r, COL:D_MODEL] + y1 * inv * vd_ref[1:2, COL:D_MODEL]
    _for_rows(tm, rc, go)


def _ln_rows(src_ref, dst_ref, g, b, tm, swish):
    rc = min(tm, ROWS_PER_NORM_TRIP)
    def go(r0):
        r = pl.ds(r0, rc)
        y = _layernorm_rows(src_ref[r, :], g, b)
        dst_ref[r, :] = _silu(y) if swish else y
    _for_rows(tm, rc, go)


def _run_tile_layer(weights, x_ref, vd_ref, out_ref,
                    lhs_ref, p_ref, s1_ref, su_ref, m_ref, wbuf_ref, w2buf_ref, wsem, w2sem,
                    tm, branch_steps, proj_dst, wait_early):
    w_in_hbm, w_out_hbm = weights[0], weights[1]
    w_br_hbm = weights[2:]
    i, l = pl.program_id(0), pl.program_id(1)
    n_tiles, depth = pl.num_programs(0), pl.num_programs(1)
    is_first = jnp.logical_and(i == 0, l == 0)
    is_last = jnp.logical_and(i == n_tiles - 1, l == depth - 1)
    next_l = jnp.where(l + 1 < depth, l + 1, 0)

    ring_base = lax.rem((i * depth + l) * N_BLOCKS, N_SLOTS)

    def slot_of(k):
        return lax.rem(ring_base + k, N_SLOTS)

    def w_copy(layer, k):
        slot = slot_of(k)
        kk = k % N_BLOCKS
        src, kk = (w_in_hbm, kk) if kk < N_IN_BLOCKS else (w_out_hbm, kk - N_IN_BLOCKS)
        return pltpu.make_async_copy(src.at[layer, :, pl.ds(kk * COL, COL)], wbuf_ref.at[slot], wsem.at[slot])

    def w2_copy(j):
        slot = j % N_W2_SLOTS
        return pltpu.make_async_copy(w_br_hbm[j // 2].at[l, :, pl.ds((j % 2) * COL, COL)],
                                     w2buf_ref.at[slot], w2sem.at[slot])

    def start_block(k):
        if k < N_BLOCKS:
            w_copy(l, k).start()
        else:
            @pl.when(jnp.logical_not(is_last))
            def _():
                w_copy(next_l, k).start()

    ahead = N_SLOTS if wait_early else LOOKAHEAD

    @pl.when(is_first)
    def _():
        for k in range(ahead):
            w_copy(l, k).start()

    if wait_early:
        w_copy(l, 0).wait()
    _start_layer(l, x_ref, vd_ref, out_ref, lhs_ref, tm)

    for k in range(N_BLOCKS):
        if not wait_early:
            w_copy(l, k).wait()
            start_block(k + ahead)
        if k == FIRST_MERGE_BLOCK - 1:
            w2_copy(0).start()

        proj = jnp.dot(lhs_ref[...], wbuf_ref[slot_of(k)], preferred_element_type=f32)
        if k in proj_dst:
            proj_dst[k][...] = proj
        else:
            p_ref[k % 2] = proj

        if wait_early:
            if k + 1 < N_BLOCKS:
                w_copy(l, k + 1).wait()
            start_block(k + ahead)

        if k in branch_steps:
            branch_steps[k]()
        if k in (2, 5, 7):
            _widen(p_ref.at[k % 2], s1_ref, su_ref.at[(2, 5, 7).index(k)], tm)
        if FIRST_MERGE_BLOCK <= k < N_IN_BLOCKS:
            j = k - FIRST_MERGE_BLOCK
            w2_copy(j).wait()
            if j + 1 < N_MERGE:
                w2_copy(j + 1).start()
            _merge(p_ref.at[k % 2], su_ref.at[j // 2], w2buf_ref.at[j % N_W2_SLOTS], m_ref, j % 2, first=j < 2)
        if k == N_IN_BLOCKS - 1:
            _m_to_lhs(m_ref, lhs_ref, tm)
        if k == N_BLOCKS - 1:
            _post_norm_residual(p_ref.at[0], p_ref.at[1], vd_ref, out_ref, tm)


def _conv_rows_prompt(zbuf_ref, cw_ref, ve_ref, dst_ref, r0, rc):
    for cb in range(E // LANES):
        cols = slice(cb * LANES, (cb + 1) * LANES)
        blk = zbuf_ref[pl.ds(r0, rc + CONV_HALO), cols]
        lead = CONV_HALO - (CONV_W - 1)
        y = ve_ref[0:1, cols] + cw_ref[CONV_W - 1:CONV_W, cols] * blk[CONV_HALO:CONV_HALO + rc, :]
        for b in range(SUBLANES):
            acc = None
            for a in range(CONV_HALO // SUBLANES):
                k = SUBLANES * a + b - lead
                if k < 0:
                    continue
                term = cw_ref[k:k + 1, cols] * blk[SUBLANES * a:SUBLANES * a + rc + SUBLANES, :]
                acc = term if acc is None else acc + term
            y = y + acc[b:b + rc, :]
        dst_ref[pl.ds(r0, rc), cols] = y


def _trailing_sum(blk, w):
    rows = blk.shape[0]
    win, lo, span = blk, 0, 1
    while span < w:
        n = rows - (lo + SUBLANES)
        win = win[SUBLANES:SUBLANES + n, :] + win[SUBLANES - span:SUBLANES - span + n, :]
        lo += SUBLANES
        span *= 2
    return win[POOL_HALO - lo:, :]


def _prompt_kernel(x_ref, w_in_hbm, w_out_hbm, w_br_a_hbm, w_br_b_hbm, w_br_c_hbm,
                   vd_ref, ve_ref, cw_ref, ws_ref, bs_ref, pw_ref,
                   out_ref, cst_ref, pst_ref,
                   lhs_ref, p_ref, s1_ref, su_ref, m_ref, wbuf_ref, w2buf_ref, wsem, w2sem,
                   zbuf_ref, cbuf_ref, zhalo_ref, chalo_ref,
                   *, tiles_per_seq):
    tm = TM_PROMPT
    l = pl.program_id(1)
    q = pl.program_id(0) % tiles_per_seq

    def conv_branch():
        @pl.when(q == 0)
        def _():
            zbuf_ref[0:CONV_HALO, :] = jnp.zeros((CONV_HALO, E), f32)

        @pl.when(q > 0)
        def _():
            zbuf_ref[0:CONV_HALO, :] = zhalo_ref[l]

        def glu(r0):
            r = pl.ds(r0, ROWS_PER_TRIP)
            zbuf_ref[pl.ds(CONV_HALO + r0, ROWS_PER_TRIP), :] = p_ref[0, r, :] * _sigmoid(p_ref[1, r, :])
        _for_rows(tm, ROWS_PER_TRIP, glu)

        _for_rows(tm, CHUNK, lambda r0: _conv_rows_prompt(zbuf_ref, cw_ref, ve_ref, s1_ref, r0, CHUNK),
                  rolled=True)
        _ln_rows(s1_ref, s1_ref, ve_ref[1:2, :], ve_ref[2:3, :], tm, swish=True)

        cst_ref[...] = zbuf_ref[CONV_HALO + tm - (CONV_W - 1):CONV_HALO + tm, :]
        zhalo_ref[l] = zbuf_ref[tm:tm + CONV_HALO, :]

    def gmlp_branch():
        _ln_rows(p_ref.at[0], s1_ref, ve_ref[3:4, :], ve_ref[4:5, :], tm, swish=False)
        tril = (lax.broadcasted_iota(jnp.int32, (CHUNK, CHUNK), 0)
                >= lax.broadcasted_iota(jnp.int32, (CHUNK, CHUNK), 1))
        for grp in range(B_GROUPS):
            cols = slice(grp * B_GC, (grp + 1) * B_GC)
            w_g = jnp.where(tril, ws_ref[grp], 0.0).astype(bf16)
            for c in range(tm // CHUNK):
                rows = slice(c * CHUNK, (c + 1) * CHUNK)
                v = s1_ref[rows, cols].astype(bf16)
                mixed = jnp.dot(w_g, v, preferred_element_type=f32) + bs_ref[:, cols]
                s1_ref[rows, cols] = p_ref[1, rows, cols] * mixed

    def pool_branch():
        @pl.when(q == 0)
        def _():
            cbuf_ref[0:POOL_HALO, :] = jnp.zeros((POOL_HALO, E), f32)

        @pl.when(q > 0)
        def _():
            cbuf_ref[0:POOL_HALO, :] = chalo_ref[l]

        def pool(r0, full_windows):
            pos = q * tm + r0 + lax.broadcasted_iota(jnp.int32, (CHUNK, C_GC), 0)
            for gi, w in enumerate(POOL_WINDOWS):
                cols = slice(gi * C_GC, (gi + 1) * C_GC)
                blk = cbuf_ref[pl.ds(r0, POOL_HALO + CHUNK), cols]
                c = blk[POOL_HALO:, :]
                if full_windows:
                    mean = _trailing_sum(blk, w) * (1.0 / w)
                else:
                    mean = _trailing_sum(blk, w) / jnp.minimum(w, pos + 1).astype(f32)
                y = jnp.dot((mean - c).astype(bf16), pw_ref[gi], preferred_element_type=f32)
                s1_ref[pl.ds(r0, CHUNK), cols] = y * ve_ref[5:6, cols]

        def pool_chunk(r0):
            if r0 > 0:
                pool(r0, full_windows=True)
            else:
                pl.when(q == 0)(lambda: pool(r0, full_windows=False))
                pl.when(q > 0)(lambda: pool(r0, full_windows=True))
        assert CHUNK >= POOL_MAX
        _for_rows(tm, CHUNK, pool_chunk)

        pst_ref[...] = cbuf_ref[POOL_HALO + tm - (POOL_MAX - 1):POOL_HALO + tm, :]
        chalo_ref[l] = cbuf_ref[tm:tm + POOL_HALO, :]

    _run_tile_layer((w_in_hbm, w_out_hbm, w_br_a_hbm, w_br_b_hbm, w_br_c_hbm), x_ref, vd_ref, out_ref,
                    lhs_ref, p_ref, s1_ref, su_ref, m_ref, wbuf_ref, w2buf_ref, wsem, w2sem,
                    tm, {1: conv_branch, 4: gmlp_branch, 6: pool_branch},
                    proj_dst={6: cbuf_ref.at[pl.ds(POOL_HALO, tm), :]}, wait_early=True)


def _sample_kernel(x_ref, w_in_hbm, w_out_hbm, w_br_a_hbm, w_br_b_hbm, w_br_c_hbm,
                   vd_ref, ve_ref, cw_ref, mix_ref, b8_ref, pw_ref, stc_ref, stp_ref,
                   out_ref, cso_ref, pso_ref, v_ref,
                   lhs_ref, p_ref, s1_ref, su_ref, m_ref, wbuf_ref, w2buf_ref, wsem, w2sem):
    tm = TM_SAMPLE
    sq = tm // DEC_SEQ
    cbw = 256

    def slab(t):
        return slice(t * sq, (t + 1) * sq)

    def conv_branch():
        def glu(r0):
            r = pl.ds(r0, ROWS_PER_TRIP)
            m_ref[r, 0:E] = p_ref[0, r, :] * _sigmoid(p_ref[1, r, :])
        _for_rows(tm, ROWS_PER_TRIP, glu)

        def zp(j, cols):
            if j < CONV_W - 1:
                return stc_ref[j, :, cols]
            return m_ref[slab(j - (CONV_W - 1)), cols]

        for j in range(CONV_W - 1):
            cso_ref[j] = zp(j + DEC_SEQ, slice(0, E))
        for cb in range(E // cbw):
            cols = slice(cb * cbw, (cb + 1) * cbw)
            for t in range(DEC_SEQ):
                acc = jnp.broadcast_to(ve_ref[0:1, cols], (sq, cbw))
                for k in range(CONV_W):
                    acc = acc + cw_ref[k:k + 1, cols] * zp(t + k, cols)
                s1_ref[slab(t), cols] = acc
        _ln_rows(s1_ref, s1_ref, ve_ref[1:2, :], ve_ref[2:3, :], tm, swish=True)

    def gmlp_branch():
        _ln_rows(p_ref.at[0], m_ref.at[:, 0:E], ve_ref[3:4, :], ve_ref[4:5, :], tm, swish=False)
        for t in range(DEC_SEQ):
            v_ref[t] = m_ref[slab(t), 0:E]
        for cb in range(E // cbw):
            cols = slice(cb * cbw, (cb + 1) * cbw)
            for t in range(DEC_SEQ):
                acc = jnp.broadcast_to(b8_ref[t:t + 1, cols], (sq, cbw))
                for s in range(t + 1):
                    acc = acc + mix_ref[t, s:s + 1, cols] * m_ref[slab(s), cols]
                s1_ref[slab(t), cols] = p_ref[1, slab(t), cols] * acc

    def pool_branch():
        def zc(j, cols):
            if j < POOL_MAX - 1:
                return stp_ref[j, :, cols]
            return p_ref[0, slab(j - (POOL_MAX - 1)), cols]

        for j in range(POOL_MAX - 1):
            pso_ref[j] = zc(j + DEC_SEQ, slice(0, E))
        for gi, w in enumerate(POOL_WINDOWS):
            cols = slice(gi * C_GC, (gi + 1) * C_GC)
            for t in range(DEC_SEQ):
                c = zc(POOL_MAX - 1 + t, cols)
                win = c
                for j in range(1, w):
                    win = win + zc(POOL_MAX - 1 + t - j, cols)
                s1_ref[slab(t), cols] = win / float(min(w, PAST_LEN + 1)) - c
        for gi in range(len(POOL_WINDOWS)):
            cols = slice(gi * C_GC, (gi + 1) * C_GC)
            y = jnp.dot(s1_ref[:, cols].astype(bf16), pw_ref[gi], preferred_element_type=f32)
            s1_ref[:, cols] = y * ve_ref[5:6, cols]

    _run_tile_layer((w_in_hbm, w_out_hbm, w_br_a_hbm, w_br_b_hbm, w_br_c_hbm), x_ref, vd_ref, out_ref,
                    lhs_ref, p_ref, s1_ref, su_ref, m_ref, wbuf_ref, w2buf_ref, wsem, w2sem,
                    tm, {1: conv_branch, 4: gmlp_branch, 6: pool_branch}, proj_dst={},
                    wait_early=False)


def _weight_specs():
    return [pl.BlockSpec(memory_space=pl.ANY)] * 5 + [
        pl.BlockSpec((None, 2, D_MODEL), lambda i, l: (l, 0, 0)),
        pl.BlockSpec((None, 6, E), lambda i, l: (l, 0, 0)),
        pl.BlockSpec((None, CONV_W, E), lambda i, l: (l, 0, 0)),
    ]


def _work_buffers(tm):
    return [
        pltpu.VMEM((tm, D_MODEL), bf16),
        pltpu.VMEM((2, tm, COL), f32),
        pltpu.VMEM((tm, E), f32),
        pltpu.VMEM((N_BRANCH, tm, E), bf16),
        pltpu.VMEM((tm, D_MODEL), f32),
        pltpu.VMEM((N_SLOTS, D_MODEL, COL), bf16),
        pltpu.VMEM((N_W2_SLOTS, E, COL), bf16),
        pltpu.SemaphoreType.DMA((N_SLOTS,)),
        pltpu.SemaphoreType.DMA((N_W2_SLOTS,)),
    ]


def _compiler_params():
    return pltpu.CompilerParams(dimension_semantics=("arbitrary", "arbitrary"),
                                vmem_limit_bytes=VMEM_LIMIT_BYTES)


def _prompt_group(depth, x2d, n_seq, seq_len, weights, vd, ve, cw, ws, bs_full, pw):
    tm = TM_PROMPT
    assert seq_len % tm == 0 and tm % CHUNK == 0
    tiles_per_seq = seq_len // tm
    n_rows = n_seq * seq_len
    grid = (n_rows // tm, depth)
    in_specs = [pl.BlockSpec((tm, D_MODEL), lambda i, l: (i, 0))] + _weight_specs() + [
        pl.BlockSpec((None, B_GROUPS, CHUNK, CHUNK), lambda i, l: (l, 0, 0, 0)),
        pl.BlockSpec((None, CHUNK, E), lambda i, l: (l, 0, 0)),
        pl.BlockSpec((None, len(POOL_WINDOWS), C_GC, C_GC), lambda i, l: (l, 0, 0, 0)),
    ]
    out_specs = [
        pl.BlockSpec((tm, D_MODEL), lambda i, l: (i, 0)),
        pl.BlockSpec((None, None, CONV_W - 1, E), lambda i, l: (l, i, 0, 0)),
        pl.BlockSpec((None, None, POOL_MAX - 1, E), lambda i, l: (l, i, 0, 0)),
    ]
    out_shape = [
        jax.ShapeDtypeStruct((n_rows, D_MODEL), f32),
        jax.ShapeDtypeStruct((depth, n_rows // tm, CONV_W - 1, E), f32),
        jax.ShapeDtypeStruct((depth, n_rows // tm, POOL_MAX - 1, E), f32),
    ]
    scratch = _work_buffers(tm) + [
        pltpu.VMEM((CONV_HALO + tm, E), f32),
        pltpu.VMEM((POOL_HALO + tm, E), f32),
        pltpu.VMEM((depth, CONV_HALO, E), f32),
        pltpu.VMEM((depth, POOL_HALO, E), f32),
    ]
    x_out, conv_tiles, pool_tiles = pl.pallas_call(
        functools.partial(_prompt_kernel, tiles_per_seq=tiles_per_seq),
        grid=grid, in_specs=in_specs, out_specs=out_specs, out_shape=out_shape,
        scratch_shapes=scratch, compiler_params=_compiler_params(),
        name="prompt_group",
    )(x2d, *weights, vd, ve, cw, ws, bs_full, pw)
    last = slice(tiles_per_seq - 1, None, tiles_per_seq)
    return x_out, conv_tiles[:, last], pool_tiles[:, last]


def _sample_group(depth, x_sample, weights, vd, ve, cw, mix, b8, pw, state_conv, state_pool):
    tm = TM_SAMPLE
    n_dec = x_sample.shape[0]
    sq = tm // DEC_SEQ
    assert n_dec % sq == 0
    n_tiles = n_dec // sq
    x2d = jnp.transpose(x_sample.reshape(n_tiles, sq, DEC_SEQ, D_MODEL), (0, 2, 1, 3)).reshape(n_dec * DEC_SEQ, D_MODEL)
    stc = jnp.transpose(state_conv, (0, 2, 1, 3))
    stp = jnp.transpose(state_pool, (0, 2, 1, 3))
    grid = (n_tiles, depth)
    in_specs = [pl.BlockSpec((tm, D_MODEL), lambda i, l: (i, 0))] + _weight_specs() + [
        pl.BlockSpec((None, DEC_SEQ, DEC_SEQ, E), lambda i, l: (l, 0, 0, 0)),
        pl.BlockSpec((None, DEC_SEQ, E), lambda i, l: (l, 0, 0)),
        pl.BlockSpec((None, len(POOL_WINDOWS), C_GC, C_GC), lambda i, l: (l, 0, 0, 0)),
        pl.BlockSpec((None, CONV_W - 1, sq, E), lambda i, l: (l, 0, i, 0)),
        pl.BlockSpec((None, POOL_MAX - 1, sq, E), lambda i, l: (l, 0, i, 0)),
    ]
    out_specs = [
        pl.BlockSpec((tm, D_MODEL), lambda i, l: (i, 0)),
        pl.BlockSpec((None, CONV_W - 1, sq, E), lambda i, l: (l, 0, i, 0)),
        pl.BlockSpec((None, POOL_MAX - 1, sq, E), lambda i, l: (l, 0, i, 0)),
        pl.BlockSpec((None, DEC_SEQ, sq, E), lambda i, l: (l, 0, i, 0)),
    ]
    out_shape = [
        jax.ShapeDtypeStruct((n_dec * DEC_SEQ, D_MODEL), f32),
        jax.ShapeDtypeStruct((depth, CONV_W - 1, n_dec, E), f32),
        jax.ShapeDtypeStruct((depth, POOL_MAX - 1, n_dec, E), f32),
        jax.ShapeDtypeStruct((depth, DEC_SEQ, n_dec, E), f32),
    ]
    x_out, conv_s, pool_s, v_s = pl.pallas_call(
        _sample_kernel,
        grid=grid, in_specs=in_specs, out_specs=out_specs, out_shape=out_shape,
        scratch_shapes=_work_buffers(tm), compiler_params=_compiler_params(),
        name="sample_group",
    )(x2d, *weights, vd, ve, cw, mix, b8, pw, stc, stp)
    x_out = jnp.transpose(x_out.reshape(n_tiles, DEC_SEQ, sq, D_MODEL), (0, 2, 1, 3)).reshape(n_dec, DEC_SEQ, D_MODEL)
    back = lambda a: jnp.transpose(a, (0, 2, 1, 3))
    return x_out, back(conv_s), back(pool_s), back(v_s)


def _sample_mix_tables(gmlp_ws, gmlp_bs):
    mix = jnp.repeat(jnp.transpose(gmlp_ws[:, :, :DEC_SEQ, :DEC_SEQ], (0, 2, 3, 1)), B_GC, axis=-1)
    b8 = jnp.repeat(jnp.transpose(gmlp_bs[:, :, :DEC_SEQ], (0, 2, 1)), B_GC, axis=-1)
    return mix, b8


def kernel(x_prompt, x_sample, state_conv, state_pool, g_pre, w_in, conv_w, conv_b, conv_ln_g, conv_ln_b,
           w_br_a, gmlp_ln_g, gmlp_ln_b, gmlp_ws, gmlp_bs, w_br_b, pool_w, pool_scale, w_br_c, w_out, g_post):
    depth = w_in.shape[0]
    n_seq, seq_len, _ = x_prompt.shape
    n_dec, dec_seq, _ = x_sample.shape
    assert dec_seq == DEC_SEQ and x_prompt.shape[2] == D_MODEL

    weights = tuple(w.astype(bf16) for w in (w_in, w_out, w_br_a, w_br_b, w_br_c))
    vd = jnp.stack([g_pre, g_post], axis=1)
    ve = jnp.stack([conv_b, conv_ln_g, conv_ln_b, gmlp_ln_g, gmlp_ln_b, pool_scale], axis=1)
    bs_full = jnp.repeat(jnp.transpose(gmlp_bs, (0, 2, 1)), B_GC, axis=-1)
    pw = pool_w.astype(bf16)
    mix, b8 = _sample_mix_tables(gmlp_ws, gmlp_bs)

    xp, conv_p, pool_p = _prompt_group(depth, x_prompt.reshape(n_seq * seq_len, D_MODEL), n_seq, seq_len,
                                       weights, vd, ve, conv_w, gmlp_ws, bs_full, pw)
    xs, conv_s, pool_s, v_s = _sample_group(depth, x_sample, weights, vd, ve, conv_w, mix, b8, pw,
                                            state_conv, state_pool)
    return (xp.reshape(n_seq, seq_len, D_MODEL), xs, conv_p, pool_p, conv_s, pool_s, v_s)
```

```python
import functools

import jax
import jax.numpy as jnp
import numpy as np
from jax import lax
from jax.experimental import pallas as pl
from jax.experimental.pallas import tpu as pltpu

D_MODEL = 2048
E = 1024
N_BRANCH = 3
CONV_W = 31
CONV_HALO = 32
SUBLANES = 8
LANES = 128
CHUNK = 128
B_GROUPS = 8
B_GC = E // B_GROUPS
POOL_WINDOWS = (2, 4, 8, 16)
C_GC = E // len(POOL_WINDOWS)
POOL_MAX = 16
POOL_HALO = 32
DEC_SEQ = 8
PAST_LEN = 16384
EPS = 1e-6
COL = 1024
N_IN_BLOCKS = (8 * E + N_BRANCH * D_MODEL) // COL
N_BLOCKS = N_IN_BLOCKS + D_MODEL // COL
FIRST_MERGE_BLOCK = 8
N_MERGE = 2 * N_BRANCH
LOOKAHEAD = 2
N_SLOTS = LOOKAHEAD + 1
N_W2_SLOTS = 2

TM_PROMPT = 512
TM_SAMPLE = 256
VMEM_LIMIT_BYTES = 60 * 1024 * 1024

f32 = jnp.float32
bf16 = jnp.bfloat16


ROWS_PER_TRIP = 64
ROWS_PER_NORM_TRIP = 512


def _for_rows(n_rows, rc, fn, rolled=False):
    if not rolled:
        for r0 in range(0, n_rows, rc):
            fn(r0)
        return

    def body(c, carry):
        fn(pl.multiple_of(c * rc, rc))
        return carry
    lax.fori_loop(0, n_rows // rc, body, 0)


def _sigmoid(x):
    return 0.5 * jnp.tanh(0.5 * x) + 0.5


def _silu(x):
    h = 0.5 * x
    return h * (jnp.tanh(h) + 1.0)


def _layernorm_rows(y, g, b):
    mu = jnp.mean(y, axis=-1, keepdims=True)
    yc = y - mu
    var = jnp.mean(yc * yc, axis=-1, keepdims=True)
    return yc * lax.rsqrt(var + EPS) * g + b


def _start_layer(l, x_ref, vd_ref, out_ref, lhs_ref, tm):
    @pl.when(l == 0)
    def _():
        def load(r0):
            out_ref[pl.ds(r0, ROWS_PER_TRIP), :] = x_ref[pl.ds(r0, ROWS_PER_TRIP), :]
        _for_rows(tm, ROWS_PER_TRIP, load)

    rc = min(tm, ROWS_PER_NORM_TRIP)
    def norm(r0):
        r = pl.ds(r0, rc)
        x = out_ref[r, :]
        ms = jnp.mean(x * x, axis=-1, keepdims=True)
        lhs_ref[r, :] = (x * lax.rsqrt(ms + EPS) * vd_ref[0:1, :]).astype(bf16)
    _for_rows(tm, rc, norm)


def _widen(p_ref, s1_ref, su_ref, tm):
    def go(r0):
        r = pl.ds(r0, ROWS_PER_TRIP)
        su_ref[r, :] = (s1_ref[r, :] * _silu(p_ref[r, :])).astype(bf16)
    _for_rows(tm, ROWS_PER_TRIP, go)


def _merge(p_ref, su_ref, w2_ref, m_ref, half, first):
    cols = slice(half * COL, (half + 1) * COL)
    y = jnp.dot(su_ref[...], w2_ref[...], preferred_element_type=f32)
    g = _sigmoid(p_ref[...]) * y
    if first:
        m_ref[:, cols] = g
    else:
        m_ref[:, cols] += g


def _m_to_lhs(m_ref, lhs_ref, tm):
    def go(r0):
        r = pl.ds(r0, ROWS_PER_TRIP)
        lhs_ref[r, :] = m_ref[r, :].astype(bf16)
    _for_rows(tm, ROWS_PER_TRIP, go)


def _post_norm_residual(y0_ref, y1_ref, vd_ref, x_ref, tm):
    rc = min(tm, ROWS_PER_NORM_TRIP)
    def go(r0):
        r = pl.ds(r0, rc)
        y0 = y0_ref[r, :]
        y1 = y1_ref[r, :]
        ss = jnp.sum(y0 * y0, axis=-1, keepdims=True) + jnp.sum(y1 * y1, axis=-1, keepdims=True)
        inv = lax.rsqrt(ss / D_MODEL + EPS)
        x_ref[r, 0:COL] = x_ref[r, 0:COL] + y0 * inv * vd_ref[1:2, 0:COL]
        x_ref[r, COL:D_MODEL] = x_ref[r, COL:D_MODEL] + y1 * inv * vd_ref[1:2, COL:D_MODEL]
    _for_rows(tm, rc, go)


def _ln_rows(src_ref, dst_ref, g, b, tm, swish):
    rc = min(tm, ROWS_PER_NORM_TRIP)
    def go(r0):
        r = pl.ds(r0, rc)
        y = _layernorm_rows(src_ref[r, :], g, b)
        dst_ref[r, :] = _silu(y) if swish else y
    _for_rows(tm, rc, go)


def _run_tile_layer(weights, x_ref, vd_ref, out_ref,
                    lhs_ref, p_ref, s1_ref, su_ref, m_ref, wbuf_ref, w2buf_ref, wsem, w2sem,
                    tm, branch_steps, proj_dst, wait_early, deferred_steps):
    w_in_hbm, w_out_hbm = weights[0], weights[1]
    w_br_hbm = weights[2:]
    i, l = pl.program_id(0), pl.program_id(1)
    n_tiles, depth = pl.num_programs(0), pl.num_programs(1)
    is_first = jnp.logical_and(i == 0, l == 0)
    is_last = jnp.logical_and(i == n_tiles - 1, l == depth - 1)
    next_l = jnp.where(l + 1 < depth, l + 1, 0)

    ring_base = lax.rem((i * depth + l) * N_BLOCKS, N_SLOTS)

    def slot_of(k):
        return lax.rem(ring_base + k, N_SLOTS)

    def w_copy(layer, k):
        slot = slot_of(k)
        kk = k % N_BLOCKS
        src, kk = (w_in_hbm, kk) if kk < N_IN_BLOCKS else (w_out_hbm, kk - N_IN_BLOCKS)
        return pltpu.make_async_copy(src.at[layer, :, pl.ds(kk * COL, COL)], wbuf_ref.at[slot], wsem.at[slot])

    def w2_copy(j):
        slot = j % N_W2_SLOTS
        return pltpu.make_async_copy(w_br_hbm[j // 2].at[l, :, pl.ds((j % 2) * COL, COL)],
                                     w2buf_ref.at[slot], w2sem.at[slot])

    def start_block(k):
        if k < N_BLOCKS:
            w_copy(l, k).start()
        else:
            @pl.when(jnp.logical_not(is_last))
            def _():
                w_copy(next_l, k).start()

    ahead = N_SLOTS if wait_early else LOOKAHEAD

    @pl.when(is_first)
    def _():
        for k in range(ahead):
            w_copy(l, k).start()

    if wait_early:
        w_copy(l, 0).wait()
    _start_layer(l, x_ref, vd_ref, out_ref, lhs_ref, tm)

    for k in range(N_BLOCKS):
        if not wait_early:
            w_copy(l, k).wait()
            start_block(k + ahead)
        if k == FIRST_MERGE_BLOCK - 1:
            w2_copy(0).start()

        proj = jnp.dot(lhs_ref[...], wbuf_ref[slot_of(k)], preferred_element_type=f32)
        if k in proj_dst:
            proj_dst[k][...] = proj
        else:
            p_ref[k % 2] = proj

        if wait_early:
            if k + 1 < N_BLOCKS:
                w_copy(l, k + 1).wait()
            start_block(k + ahead)

        if k in deferred_steps:
            deferred_steps[k]()
        if k in branch_steps:
            branch_steps[k]()
        if k in (2, 5, 7):
            _widen(p_ref.at[k % 2], s1_ref, su_ref.at[(2, 5, 7).index(k)], tm)
        if FIRST_MERGE_BLOCK <= k < N_IN_BLOCKS:
            j = k - FIRST_MERGE_BLOCK
            w2_copy(j).wait()
            if j + 1 < N_MERGE:
                w2_copy(j + 1).start()
            _merge(p_ref.at[k % 2], su_ref.at[j // 2], w2buf_ref.at[j % N_W2_SLOTS], m_ref, j % 2, first=j < 2)
        if k == N_IN_BLOCKS - 1:
            _m_to_lhs(m_ref, lhs_ref, tm)
        if k == N_BLOCKS - 1:
            _post_norm_residual(p_ref.at[0], p_ref.at[1], vd_ref, out_ref, tm)


def _conv_rows_prompt(zbuf_ref, cw_ref, ve_ref, dst_ref, r0, rc):
    for cb in range(E // LANES):
        cols = slice(cb * LANES, (cb + 1) * LANES)
        blk = zbuf_ref[pl.ds(r0, rc + CONV_HALO), cols]
        lead = CONV_HALO - (CONV_W - 1)
        y = ve_ref[0:1, cols] + cw_ref[CONV_W - 1:CONV_W, cols] * blk[CONV_HALO:CONV_HALO + rc, :]
        for b in range(SUBLANES):
            acc = None
            for a in range(CONV_HALO // SUBLANES):
                k = SUBLANES * a + b - lead
                if k < 0:
                    continue
                term = cw_ref[k:k + 1, cols] * blk[SUBLANES * a:SUBLANES * a + rc + SUBLANES, :]
                acc = term if acc is None else acc + term
            y = y + acc[b:b + rc, :]
        dst_ref[pl.ds(r0, rc), cols] = y


def _trailing_sum(blk, w):
    rows = blk.shape[0]
    win, lo, span = blk, 0, 1
    while span < w:
        n = rows - (lo + SUBLANES)
        win = win[SUBLANES:SUBLANES + n, :] + win[SUBLANES - span:SUBLANES - span + n, :]
        lo += SUBLANES
        span *= 2
    return win[POOL_HALO - lo:, :]


def _prompt_kernel(x_ref, w_in_hbm, w_out_hbm, w_br_a_hbm, w_br_b_hbm, w_br_c_hbm,
                   vd_ref, ve_ref, cw_ref, ws_ref, bs_ref, pw_ref,
                   out_ref, cst_ref, pst_ref,
                   lhs_ref, p_ref, s1_ref, su_ref, m_ref, wbuf_ref, w2buf_ref, wsem, w2sem,
                   zbuf_ref, cbuf_ref, zhalo_ref, chalo_ref,
                   *, tiles_per_seq):
    tm = TM_PROMPT
    l = pl.program_id(1)
    q = pl.program_id(0) % tiles_per_seq

    def glu_branch():
        @pl.when(q == 0)
        def _():
            zbuf_ref[0:CONV_HALO, :] = jnp.zeros((CONV_HALO, E), f32)

        @pl.when(q > 0)
        def _():
            zbuf_ref[0:CONV_HALO, :] = zhalo_ref[l]

        def glu(r0):
            r = pl.ds(r0, ROWS_PER_TRIP)
            zbuf_ref[pl.ds(CONV_HALO + r0, ROWS_PER_TRIP), :] = p_ref[0, r, :] * _sigmoid(p_ref[1, r, :])
        _for_rows(tm, ROWS_PER_TRIP, glu)

    def conv_branch():
        _for_rows(tm, CHUNK, lambda r0: _conv_rows_prompt(zbuf_ref, cw_ref, ve_ref, s1_ref, r0, CHUNK),
                  rolled=True)
        _ln_rows(s1_ref, s1_ref, ve_ref[1:2, :], ve_ref[2:3, :], tm, swish=True)

        cst_ref[...] = zbuf_ref[CONV_HALO + tm - (CONV_W - 1):CONV_HALO + tm, :]
        zhalo_ref[l] = zbuf_ref[tm:tm + CONV_HALO, :]

    def gmlp_branch():
        _ln_rows(p_ref.at[0], s1_ref, ve_ref[3:4, :], ve_ref[4:5, :], tm, swish=False)
        tril = (lax.broadcasted_iota(jnp.int32, (CHUNK, CHUNK), 0)
                >= lax.broadcasted_iota(jnp.int32, (CHUNK, CHUNK), 1))
        for grp in range(B_GROUPS):
            cols = slice(grp * B_GC, (grp + 1) * B_GC)
            w_g = jnp.where(tril, ws_ref[grp], 0.0).astype(bf16)
            for c in range(tm // CHUNK):
                rows = slice(c * CHUNK, (c + 1) * CHUNK)
                v = s1_ref[rows, cols].astype(bf16)
                mixed = jnp.dot(w_g, v, preferred_element_type=f32) + bs_ref[:, cols]
                s1_ref[rows, cols] = p_ref[1, rows, cols] * mixed

    def pool_branch():
        @pl.when(q == 0)
        def _():
            cbuf_ref[0:POOL_HALO, :] = jnp.zeros((POOL_HALO, E), f32)

        @pl.when(q > 0)
        def _():
            cbuf_ref[0:POOL_HALO, :] = chalo_ref[l]

        def pool(r0, full_windows):
            pos = q * tm + r0 + lax.broadcasted_iota(jnp.int32, (CHUNK, C_GC), 0)
            for gi, w in enumerate(POOL_WINDOWS):
                cols = slice(gi * C_GC, (gi + 1) * C_GC)
                blk = cbuf_ref[pl.ds(r0, POOL_HALO + CHUNK), cols]
                c = blk[POOL_HALO:, :]
                if full_windows:
                    mean = _trailing_sum(blk, w) * (1.0 / w)
                else:
                    mean = _trailing_sum(blk, w) / jnp.minimum(w, pos + 1).astype(f32)
                y = jnp.dot((mean - c).astype(bf16), pw_ref[gi], preferred_element_type=f32)
                s1_ref[pl.ds(r0, CHUNK), cols] = y * ve_ref[5:6, cols]

        def pool_chunk(r0):
            if r0 > 0:
                pool(r0, full_windows=True)
            else:
                pl.when(q == 0)(lambda: pool(r0, full_windows=False))
                pl.when(q > 0)(lambda: pool(r0, full_windows=True))
        assert CHUNK >= POOL_MAX
        _for_rows(tm, CHUNK, pool_chunk)

        pst_ref[...] = cbuf_ref[POOL_HALO + tm - (POOL_MAX - 1):POOL_HALO + tm, :]
        chalo_ref[l] = cbuf_ref[tm:tm + POOL_HALO, :]

    _run_tile_layer((w_in_hbm, w_out_hbm, w_br_a_hbm, w_br_b_hbm, w_br_c_hbm), x_ref, vd_ref, out_ref,
                    lhs_ref, p_ref, s1_ref, su_ref, m_ref, wbuf_ref, w2buf_ref, wsem, w2sem,
                    tm, {1: glu_branch, 4: gmlp_branch, 6: pool_branch},
                    proj_dst={6: cbuf_ref.at[pl.ds(POOL_HALO, tm), :]}, wait_early=True,
                    deferred_steps={2: conv_branch})


def _sample_kernel(x_ref, w_in_hbm, w_out_hbm, w_br_a_hbm, w_br_b_hbm, w_br_c_hbm,
                   vd_ref, ve_ref, cw_ref, mix_ref, b8_ref, pw_ref, stc_ref, stp_ref,
                   out_ref, cso_ref, pso_ref, v_ref,
                   lhs_ref, p_ref, s1_ref, su_ref, m_ref, wbuf_ref, w2buf_ref, wsem, w2sem):
    tm = TM_SAMPLE
    sq = tm // DEC_SEQ
    cbw = 256

    def slab(t):
        return slice(t * sq, (t + 1) * sq)

    def conv_branch():
        def glu(r0):
            r = pl.ds(r0, ROWS_PER_TRIP)
            m_ref[r, 0:E] = p_ref[0, r, :] * _sigmoid(p_ref[1, r, :])
        _for_rows(tm, ROWS_PER_TRIP, glu)

        def zp(j, cols):
            if j < CONV_W - 1:
                return stc_ref[j, :, cols]
            return m_ref[slab(j - (CONV_W - 1)), cols]

        for j in range(CONV_W - 1):
            cso_ref[j] = zp(j + DEC_SEQ, slice(0, E))
        for cb in range(E // cbw):
            cols = slice(cb * cbw, (cb + 1) * cbw)
            for t in range(DEC_SEQ):
                acc = jnp.broadcast_to(ve_ref[0:1, cols], (sq, cbw))
                for k in range(CONV_W):
                    acc = acc + cw_ref[k:k + 1, cols] * zp(t + k, cols)
                s1_ref[slab(t), cols] = acc
        _ln_rows(s1_ref, s1_ref, ve_ref[1:2, :], ve_ref[2:3, :], tm, swish=True)

    def gmlp_branch():
        _ln_rows(p_ref.at[0], m_ref.at[:, 0:E], ve_ref[3:4, :], ve_ref[4:5, :], tm, swish=False)
        for t in range(DEC_SEQ):
            v_ref[t] = m_ref[slab(t), 0:E]
        for cb in range(E // cbw):
            cols = slice(cb * cbw, (cb + 1) * cbw)
            for t in range(DEC_SEQ):
                acc = jnp.broadcast_to(b8_ref[t:t + 1, cols], (sq, cbw))
                for s in range(t + 1):
                    acc = acc + mix_ref[t, s:s + 1, cols] * m_ref[slab(s), cols]
                s1_ref[slab(t), cols] = p_ref[1, slab(t), cols] * acc

    def pool_branch():
        def zc(j, cols):
            if j < POOL_MAX - 1:
                return stp_ref[j, :, cols]
            return p_ref[0, slab(j - (POOL_MAX - 1)), cols]

        for j in range(POOL_MAX - 1):
            pso_ref[j] = zc(j + DEC_SEQ, slice(0, E))
        for gi, w in enumerate(POOL_WINDOWS):
            cols = slice(gi * C_GC, (gi + 1) * C_GC)
            for t in range(DEC_SEQ):
                c = zc(POOL_MAX - 1 + t, cols)
                win = c
                for j in range(1, w):
                    win = win + zc(POOL_MAX - 1 + t - j, cols)
                s1_ref[slab(t), cols] = win / float(min(w, PAST_LEN + 1)) - c
        for gi in range(len(POOL_WINDOWS)):
            cols = slice(gi * C_GC, (gi + 1) * C_GC)
            y = jnp.dot(s1_ref[:, cols].astype(bf16), pw_ref[gi], preferred_element_type=f32)
            s1_ref[:, cols] = y * ve_ref[5:6, cols]

    _run_tile_layer((w_in_hbm, w_out_hbm, w_br_a_hbm, w_br_b_hbm, w_br_c_hbm), x_ref, vd_ref, out_ref,
                    lhs_ref, p_ref, s1_ref, su_ref, m_ref, wbuf_ref, w2buf_ref, wsem, w2sem,
                    tm, {1: conv_branch, 4: gmlp_branch, 6: pool_branch}, proj_dst={},
                    wait_early=False,
                    deferred_steps={})


def _weight_specs():
    return [pl.BlockSpec(memory_space=pl.ANY)] * 5 + [
        pl.BlockSpec((None, 2, D_MODEL), lambda i, l: (l, 0, 0)),
        pl.BlockSpec((None, 6, E), lambda i, l: (l, 0, 0)),
        pl.BlockSpec((None, CONV_W, E), lambda i, l: (l, 0, 0)),
    ]


def _work_buffers(tm):
    return [
        pltpu.VMEM((tm, D_MODEL), bf16),
        pltpu.VMEM((2, tm, COL), f32),
        pltpu.VMEM((tm, E), f32),
        pltpu.VMEM((N_BRANCH, tm, E), bf16),
        pltpu.VMEM((tm, D_MODEL), f32),
        pltpu.VMEM((N_SLOTS, D_MODEL, COL), bf16),
        pltpu.VMEM((N_W2_SLOTS, E, COL), bf16),
        pltpu.SemaphoreType.DMA((N_SLOTS,)),
        pltpu.SemaphoreType.DMA((N_W2_SLOTS,)),
    ]


def _compiler_params():
    return pltpu.CompilerParams(dimension_semantics=("arbitrary", "arbitrary"),
                                vmem_limit_bytes=VMEM_LIMIT_BYTES)


def _prompt_group(depth, x2d, n_seq, seq_len, weights, vd, ve, cw, ws, bs_full, pw):
    tm = TM_PROMPT
    assert seq_len % tm == 0 and tm % CHUNK == 0
    tiles_per_seq = seq_len // tm
    n_rows = n_seq * seq_len
    grid = (n_rows // tm, depth)
    in_specs = [pl.BlockSpec((tm, D_MODEL), lambda i, l: (i, 0))] + _weight_specs() + [
        pl.BlockSpec((None, B_GROUPS, CHUNK, CHUNK), lambda i, l: (l, 0, 0, 0)),
        pl.BlockSpec((None, CHUNK, E), lambda i, l: (l, 0, 0)),
        pl.BlockSpec((None, len(POOL_WINDOWS), C_GC, C_GC), lambda i, l: (l, 0, 0, 0)),
    ]
    out_specs = [
        pl.BlockSpec((tm, D_MODEL), lambda i, l: (i, 0)),
        pl.BlockSpec((None, None, CONV_W - 1, E), lambda i, l: (l, i, 0, 0)),
        pl.BlockSpec((None, None, POOL_MAX - 1, E), lambda i, l: (l, i, 0, 0)),
    ]
    out_shape = [
        jax.ShapeDtypeStruct((n_rows, D_MODEL), f32),
        jax.ShapeDtypeStruct((depth, n_rows // tm, CONV_W - 1, E), f32),
        jax.ShapeDtypeStruct((depth, n_rows // tm, POOL_MAX - 1, E), f32),
    ]
    scratch = _work_buffers(tm) + [
        pltpu.VMEM((CONV_HALO + tm, E), f32),
        pltpu.VMEM((POOL_HALO + tm, E), f32),
        pltpu.VMEM((depth, CONV_HALO, E), f32),
        pltpu.VMEM((depth, POOL_HALO, E), f32),
    ]
    x_out, conv_tiles, pool_tiles = pl.pallas_call(
        functools.partial(_prompt_kernel, tiles_per_seq=tiles_per_seq),
        grid=grid, in_specs=in_specs, out_specs=out_specs, out_shape=out_shape,
        scratch_shapes=scratch, compiler_params=_compiler_params(),
        name="prompt_group",
    )(x2d, *weights, vd, ve, cw, ws, bs_full, pw)
    last = slice(tiles_per_seq - 1, None, tiles_per_seq)
    return x_out, conv_tiles[:, last], pool_tiles[:, last]


def _sample_group(depth, x_sample, weights, vd, ve, cw, mix, b8, pw, state_conv, state_pool):
    tm = TM_SAMPLE
    n_dec = x_sample.shape[0]
    sq = tm // DEC_SEQ
    assert n_dec % sq == 0
    n_tiles = n_dec // sq
    x2d = jnp.transpose(x_sample.reshape(n_tiles, sq, DEC_SEQ, D_MODEL), (0, 2, 1, 3)).reshape(n_dec * DEC_SEQ, D_MODEL)
    stc = jnp.transpose(state_conv, (0, 2, 1, 3))
    stp = jnp.transpose(state_pool, (0, 2, 1, 3))
    grid = (n_tiles, depth)
    in_specs = [pl.BlockSpec((tm, D_MODEL), lambda i, l: (i, 0))] + _weight_specs() + [
        pl.BlockSpec((None, DEC_SEQ, DEC_SEQ, E), lambda i, l: (l, 0, 0, 0)),
        pl.BlockSpec((None, DEC_SEQ, E), lambda i, l: (l, 0, 0)),
        pl.BlockSpec((None, len(POOL_WINDOWS), C_GC, C_GC), lambda i, l: (l, 0, 0, 0)),
        pl.BlockSpec((None, CONV_W - 1, sq, E), lambda i, l: (l, 0, i, 0)),
        pl.BlockSpec((None, POOL_MAX - 1, sq, E), lambda i, l: (l, 0, i, 0)),
    ]
    out_specs = [
        pl.BlockSpec((tm, D_MODEL), lambda i, l: (i, 0)),
        pl.BlockSpec((None, CONV_W - 1, sq, E), lambda i, l: (l, 0, i, 0)),
        pl.BlockSpec((None, POOL_MAX - 1, sq, E), lambda i, l: (l, 0, i, 0)),
        pl.BlockSpec((None, DEC_SEQ, sq, E), lambda i, l: (l, 0, i, 0)),
    ]
    out_shape = [
        jax.ShapeDtypeStruct((n_dec * DEC_SEQ, D_MODEL), f32),
        jax.ShapeDtypeStruct((depth, CONV_W - 1, n_dec, E), f32),
        jax.ShapeDtypeStruct((depth, POOL_MAX - 1, n_dec, E), f32),
        jax.ShapeDtypeStruct((depth, DEC_SEQ, n_dec, E), f32),
    ]
    x_out, conv_s, pool_s, v_s = pl.pallas_call(
        _sample_kernel,
        grid=grid, in_specs=in_specs, out_specs=out_specs, out_shape=out_shape,
        scratch_shapes=_work_buffers(tm), compiler_params=_compiler_params(),
        name="sample_group",
    )(x2d, *weights, vd, ve, cw, mix, b8, pw, stc, stp)
    x_out = jnp.transpose(x_out.reshape(n_tiles, DEC_SEQ, sq, D_MODEL), (0, 2, 1, 3)).reshape(n_dec, DEC_SEQ, D_MODEL)
    back = lambda a: jnp.transpose(a, (0, 2, 1, 3))
    return x_out, back(conv_s), back(pool_s), back(v_s)


def _sample_mix_tables(gmlp_ws, gmlp_bs):
    mix = jnp.repeat(jnp.transpose(gmlp_ws[:, :, :DEC_SEQ, :DEC_SEQ], (0, 2, 3, 1)), B_GC, axis=-1)
    b8 = jnp.repeat(jnp.transpose(gmlp_bs[:, :, :DEC_SEQ], (0, 2, 1)), B_GC, axis=-1)
    return mix, b8


def kernel(x_prompt, x_sample, state_conv, state_pool, g_pre, w_in, conv_w, conv_b, conv_ln_g, conv_ln_b,
           w_br_a, gmlp_ln_g, gmlp_ln_b, gmlp_ws, gmlp_bs, w_br_b, pool_w, pool_scale, w_br_c, w_out, g_post):
    depth = w_in.shape[0]
    n_seq, seq_len, _ = x_prompt.shape
    n_dec, dec_seq, _ = x_sample.shape
    assert dec_seq == DEC_SEQ and x_prompt.shape[2] == D_MODEL

    weights = tuple(w.astype(bf16) for w in (w_in, w_out, w_br_a, w_br_b, w_br_c))
    vd = jnp.stack([g_pre, g_post], axis=1)
    ve = jnp.stack([conv_b, conv_ln_g, conv_ln_b, gmlp_ln_g, gmlp_ln_b, pool_scale], axis=1)
    bs_full = jnp.repeat(jnp.transpose(gmlp_bs, (0, 2, 1)), B_GC, axis=-1)
    pw = pool_w.astype(bf16)
    mix, b8 = _sample_mix_tables(gmlp_ws, gmlp_bs)

    xp, conv_p, pool_p = _prompt_group(depth, x_prompt.reshape(n_seq * seq_len, D_MODEL), n_seq, seq_len,
                                       weights, vd, ve, conv_w, gmlp_ws, bs_full, pw)
    xs, conv_s, pool_s, v_s = _sample_group(depth, x_sample, weights, vd, ve, conv_w, mix, b8, pw,
                                            state_conv, state_pool)
    return (xp.reshape(n_seq, seq_len, D_MODEL), xs, conv_p, pool_p, conv_s, pool_s, v_s)
```
